```python
import jax, jax.numpy as jnp
from jax import lax
import numpy as np

D_MODEL = 2048
BATCH = 2
SEQ = 4096
DEPTH = 4

N_META = 16
D_INNER = 2 * D_MODEL
SSM_HEAD_DIM = 64
SSM_HEADS = D_INNER // SSM_HEAD_DIM
SSM_GROUPS = 8
SSM_STATE = 128
CONV_WIDTH = 4
CHUNK = 128
CONV_DIM = D_INNER + 2 * SSM_GROUPS * SSM_STATE
ATTN_HEADS = 16
ATTN_KV_HEADS = 4
ATTN_HEAD_DIM = 128
ATTN_WIDTH = ATTN_HEADS * ATTN_HEAD_DIM
KV_WIDTH = ATTN_KV_HEADS * ATTN_HEAD_DIM
IDX_HEADS = 16
IDX_DIM = 64
TOPK_MAX = 256
Q_BLOCK = 128
ROPE_THETA = 500000.0
ROPE_FRACTION = 4
EPS = 1e-6
IN_WIDTHS = (
    D_INNER,
    CONV_DIM,
    SSM_HEADS,
    ATTN_WIDTH,
    KV_WIDTH,
    KV_WIDTH,
    ATTN_WIDTH,
    IDX_HEADS * IDX_DIM,
    IDX_DIM,
    IDX_HEADS,
    D_MODEL,
    D_MODEL,
)
N_IN = 20624

kernel_name = "hybrid_ssd_dsa_gated_trunk"


def split_columns(proj):
    outs, start = [], 0
    for w in IN_WIDTHS:
        outs.append(proj[..., start:start + w])
        start += w
    return outs


def rms_norm(x, w):
    xf = x.astype(jnp.float32)
    y = xf * lax.rsqrt(jnp.mean(xf * xf, axis=-1, keepdims=True) + EPS)
    return (y * w.astype(jnp.float32)).astype(x.dtype)


def rope_tables(positions, head_dim):
    rot = head_dim // ROPE_FRACTION
    inv = ROPE_THETA ** (-jnp.arange(0, rot, 2, dtype=jnp.float32) / rot)
    ang = positions.astype(jnp.float32)[:, None] * inv[None, :]
    return jnp.cos(ang), jnp.sin(ang)


def apply_partial_rope(x, cos, sin):
    half = cos.shape[-1]
    rot = 2 * half
    shape = (1, cos.shape[0]) + (1,) * (x.ndim - 3) + (half,)
    c, s = cos.reshape(shape), sin.reshape(shape)
    xf = x.astype(jnp.float32)
    x1, x2, xp = xf[..., :half], xf[..., half:rot], xf[..., rot:]
    return jnp.concatenate([x1 * c - x2 * s, x2 * c + x1 * s, xp], axis=-1).astype(x.dtype)


def causal_depthwise_conv(u, w, b):
    out = lax.conv_general_dilated(
        u, w[:, None, :].astype(u.dtype), window_strides=(1,),
        padding=[(CONV_WIDTH - 1, 0)], dimension_numbers=("NWC", "WIO", "NWC"),
        feature_group_count=u.shape[-1])
    return out + b.astype(u.dtype)


def ssd_chunked(xs, dt, a, bm, cm):
    f32 = jnp.float32
    Bsz, Tp, H, P = xs.shape
    G, N = bm.shape[-2:]
    R = H // G
    nc = Tp // CHUNK
    x = xs.astype(f32).reshape(Bsz, nc, CHUNK, G, R, P)
    dtc = dt.astype(f32).reshape(Bsz, nc, CHUNK, G, R)
    bc = bm.astype(f32).reshape(Bsz, nc, CHUNK, G, N)
    cc = cm.astype(f32).reshape(Bsz, nc, CHUNK, G, N)
    xdt = x * dtc[..., None]
    acs = jnp.cumsum(dtc * a.astype(f32).reshape(G, R), axis=2)
    acs_t = jnp.moveaxis(acs, 2, -1)
    causal = jnp.tril(jnp.ones((CHUNK, CHUNK), dtype=bool))
    lmat = jnp.exp(jnp.where(causal, acs_t[..., :, None] - acs_t[..., None, :], -jnp.inf))
    cb = jnp.einsum('bclgn,bcsgn->bcgls', cc, bc)
    y_diag = jnp.einsum('bcgls,bcgrls,bcsgrp->bclgrp', cb, lmat, xdt)
    decay_to_end = jnp.exp(acs[:, :, -1:] - acs)
    chunk_states = jnp.einsum('bclgn,bclgr,bclgrp->bcgrpn', bc, decay_to_end, xdt)
    chunk_decay = jnp.exp(acs[:, :, -1])

    def step(h, inp):
        s_c, d_c = inp
        return h * d_c[..., None, None] + s_c, h

    h0 = jnp.zeros((Bsz, G, R, P, N), f32)
    _, h_in = lax.scan(step, h0, (jnp.moveaxis(chunk_states, 1, 0), jnp.moveaxis(chunk_decay, 1, 0)))
    h_in = jnp.moveaxis(h_in, 0, 1)
    y_off = jnp.einsum('bclgn,bcgrpn,bclgr->bclgrp', cc, h_in, jnp.exp(acs))
    return (y_diag + y_off).reshape(Bsz, Tp, H, P)


def ssm_branch(z, xbc, dt_raw, conv_w, conv_b, dt_bias, a_log, d_skip, norm_w):
    Bsz, T, _ = xbc.shape
    xbc = jax.nn.silu(causal_depthwise_conv(xbc, conv_w, conv_b))
    dt = jax.nn.softplus(dt_raw.astype(jnp.float32) + dt_bias.astype(jnp.float32))
    lead = CHUNK - N_META
    xbc_p = jnp.pad(xbc, ((0, 0), (lead, 0), (0, 0)))
    dt_p = jnp.pad(dt, ((0, 0), (lead, 0), (0, 0)))
    Tp = T + lead
    xs = xbc_p[..., :D_INNER].reshape(Bsz, Tp, SSM_HEADS, SSM_HEAD_DIM)
    bm = xbc_p[..., D_INNER:D_INNER + SSM_GROUPS * SSM_STATE].reshape(Bsz, Tp, SSM_GROUPS, SSM_STATE)
    cm = xbc_p[..., D_INNER + SSM_GROUPS * SSM_STATE:].reshape(Bsz, Tp, SSM_GROUPS, SSM_STATE)
    a = -jnp.exp(a_log.astype(jnp.float32))
    y = ssd_chunked(xs, dt_p, a, bm, cm)
    y = y + d_skip.astype(jnp.float32)[:, None] * xs.astype(jnp.float32)
    y = y[:, lead:].reshape(Bsz, T, D_INNER)
    g = (y * jax.nn.silu(z.astype(jnp.float32))).reshape(Bsz, T, SSM_GROUPS, D_INNER // SSM_GROUPS)
    g = rms_norm(g, norm_w.reshape(SSM_GROUPS, D_INNER // SSM_GROUPS))
    return g.reshape(Bsz, T, D_INNER).astype(z.dtype)


def indexer_sparse_attention(q, k, v, iq, ik, iw, k_sel):
    f32 = jnp.float32
    Bsz, T = q.shape[:2]
    nblk = -(-T // Q_BLOCK)
    pad = nblk * Q_BLOCK - T
    group = ATTN_HEADS // ATTN_KV_HEADS
    key_pos = jnp.arange(T)
    q_pos = jnp.arange(nblk * Q_BLOCK).reshape(nblk, Q_BLOCK)
    ikf = ik.astype(f32)
    gather = jax.vmap(lambda src, idx: src[idx])

    def to_blocks(a):
        a = jnp.pad(a, [(0, 0), (0, pad)] + [(0, 0)] * (a.ndim - 2))
        return jnp.moveaxis(a.reshape((Bsz, nblk, Q_BLOCK) + a.shape[2:]), 1, 0)

    def one_block(blk):
        qb, iqb, iwb, pos = blk
        logits = jnp.einsum('bqhd,bsd->bqhs', iqb.astype(f32), ikf)
        score = jnp.einsum('bqh,bqhs->bqs', iwb.astype(f32), jax.nn.relu(logits))
        visible = key_pos[None, :] <= pos[:, None]
        score = jnp.where(visible[None], score, -jnp.inf)
        _, sel = lax.top_k(score, k_sel)
        ok = sel <= pos[None, :, None]
        kg = gather(k, sel)
        vg = gather(v, sel)
        qg = qb.reshape(Bsz, Q_BLOCK, ATTN_KV_HEADS, group, ATTN_HEAD_DIM)
        s = jnp.einsum('bqgrd,bqkgd->bqgrk', qg, kg).astype(f32) * (ATTN_HEAD_DIM ** -0.5)
        s = jnp.where(ok[:, :, None, None, :], s, -jnp.inf)
        p = jax.nn.softmax(s, axis=-1).astype(v.dtype)
        o = jnp.einsum('bqgrk,bqkgd->bqgrd', p, vg)
        return o.reshape(Bsz, Q_BLOCK, ATTN_WIDTH)

    out = lax.map(one_block, (to_blocks(q), to_blocks(iq), to_blocks(iw), q_pos))
    out = jnp.moveaxis(out, 0, 1).reshape(Bsz, nblk * Q_BLOCK, ATTN_WIDTH)
    return out[:, :T]


def attn_branch(q, k, v, z, iq, ik, iw, q_norm_w, k_norm_w, idx_k_norm_w, cos_a, sin_a, cos_i, sin_i, k_sel):
    Bsz, T = q.shape[:2]
    q = q.reshape(Bsz, T, ATTN_HEADS, ATTN_HEAD_DIM)
    k = k.reshape(Bsz, T, ATTN_KV_HEADS, ATTN_HEAD_DIM)
    v = v.reshape(Bsz, T, ATTN_KV_HEADS, ATTN_HEAD_DIM)
    q = apply_partial_rope(rms_norm(q, q_norm_w), cos_a, sin_a)
    k = apply_partial_rope(rms_norm(k, k_norm_w), cos_a, sin_a)
    iq = apply_partial_rope(iq.reshape(Bsz, T, IDX_HEADS, IDX_DIM), cos_i, sin_i)
    ik = apply_partial_rope(rms_norm(ik, idx_k_norm_w), cos_i, sin_i)
    iw = iw * (IDX_HEADS ** -0.5 * IDX_DIM ** -0.5)
    o = indexer_sparse_attention(q, k, v, iq, ik, iw, k_sel)
    return o * jax.nn.silu(z)


def setup_inputs(seed: int = 0) -> dict:
    key = jax.random.key(seed)
    ks = jax.random.split(key, 16)
    f32 = jnp.float32

    def normal(k, shape, scale):
        return jax.random.normal(k, shape, f32) * scale

    dt0 = jnp.exp(jax.random.uniform(ks[6], (DEPTH, SSM_HEADS), f32, np.log(1e-3), np.log(1e-1)))
    return {
        "x": normal(ks[0], (BATCH, SEQ, D_MODEL), 1.0),
        "meta_tokens": normal(ks[1], (N_META, D_MODEL), 1.0),
        "norm_w": 1.0 + normal(ks[2], (DEPTH, D_MODEL), 0.02),
        "w_in": normal(ks[3], (DEPTH, D_MODEL, N_IN), D_MODEL ** -0.5),
        "conv_w": normal(ks[4], (DEPTH, CONV_WIDTH, CONV_DIM), CONV_WIDTH ** -0.5),
        "conv_b": normal(ks[5], (DEPTH, CONV_DIM), 0.02),
        "dt_bias": dt0 + jnp.log(-jnp.expm1(-dt0)),
        "a_log": jnp.log(jax.random.uniform(ks[7], (DEPTH, SSM_HEADS), f32, 1.0, 16.0)),
        "d_skip": 1.0 + normal(ks[8], (DEPTH, SSM_HEADS), 0.02),
        "ssm_norm_w": 1.0 + normal(ks[9], (DEPTH, D_INNER), 0.02),
        "w_ssm_out": normal(ks[10], (DEPTH, D_INNER, D_MODEL), D_INNER ** -0.5),
        "q_norm_w": 1.0 + normal(ks[11], (DEPTH, ATTN_HEAD_DIM), 0.02),
        "k_norm_w": 1.0 + normal(ks[12], (DEPTH, ATTN_HEAD_DIM), 0.02),
        "idx_k_norm_w": 1.0 + normal(ks[13], (DEPTH, IDX_DIM), 0.02),
        "w_attn_out": normal(ks[14], (DEPTH, ATTN_WIDTH, D_MODEL), ATTN_WIDTH ** -0.5),
        "w_out": normal(ks[15], (DEPTH, D_MODEL, D_MODEL), D_MODEL ** -0.5),
    }


def reference(x, meta_tokens, norm_w, w_in, conv_w, conv_b, dt_bias, a_log, d_skip, ssm_norm_w,
              w_ssm_out, q_norm_w, k_norm_w, idx_k_norm_w, w_attn_out, w_out):
    Bsz, seq = x.shape[0], x.shape[1]
    k_sel = min(TOPK_MAX, seq // 4)
    meta = jnp.broadcast_to(meta_tokens[None].astype(x.dtype), (Bsz, N_META, D_MODEL))
    h = jnp.concatenate([meta, x], axis=1)
    T = h.shape[1]
    positions = jnp.arange(T)
    cos_a, sin_a = rope_tables(positions, ATTN_HEAD_DIM)
    cos_i, sin_i = rope_tables(positions, IDX_DIM)
    for l in range(DEPTH):
        hn = rms_norm(h, norm_w[l])
        proj = hn @ w_in[l]
        (s_z, s_xbc, s_dt, a_q, a_k, a_v, a_z, i_q, i_k, i_w, g_s, g_a) = split_columns(proj)
        y_s = ssm_branch(s_z, s_xbc, s_dt, conv_w[l], conv_b[l], dt_bias[l], a_log[l], d_skip[l], ssm_norm_w[l])
        y_a = attn_branch(a_q, a_k, a_v, a_z, i_q, i_k, i_w, q_norm_w[l], k_norm_w[l], idx_k_norm_w[l],
                          cos_a, sin_a, cos_i, sin_i, k_sel)
        y_s = y_s @ w_ssm_out[l]
        y_a = y_a @ w_attn_out[l]
        merged = jax.nn.sigmoid(g_s) * y_s + jax.nn.sigmoid(g_a) * y_a
        h = h + merged @ w_out[l]
    return h[:, N_META:]
```

```python
import functools

import jax
import jax.numpy as jnp
from jax import lax
from jax.experimental import pallas as pl
from jax.experimental.pallas import tpu as pltpu

D_MODEL = 2048
N_META = 16
D_INNER = 2 * D_MODEL
SSM_HEAD_DIM = 64
SSM_HEADS = D_INNER // SSM_HEAD_DIM
SSM_GROUPS = 8
SSM_STATE = 128
CONV_WIDTH = 4
BC_WIDTH = SSM_GROUPS * SSM_STATE
GROUP_WIDTH = D_INNER // SSM_GROUPS
ATTN_HEADS = 16
ATTN_KV_HEADS = 4
ATTN_HEAD_DIM = 128
ATTN_WIDTH = ATTN_HEADS * ATTN_HEAD_DIM
KV_WIDTH = ATTN_KV_HEADS * ATTN_HEAD_DIM
KV_GROUP = ATTN_HEADS // ATTN_KV_HEADS
IDX_HEADS = 16
IDX_DIM = 64
IDX_WIDTH = IDX_HEADS * IDX_DIM
TOPK_MAX = 256
ROPE_THETA = 500000.0
ROPE_FRACTION = 4
EPS = 1e-6

LANES = 128
ROW_BLOCK = 128
VMEM_LIMIT = 56 * 1024 * 1024

MM = jnp.bfloat16
NEG_BIG = -1e30
INT_MIN = -2 ** 31

_SRC = {}
_o = 0
for _n, _w in (("z", D_INNER), ("x", D_INNER), ("b", BC_WIDTH), ("c", BC_WIDTH), ("dt", SSM_HEADS),
               ("q", ATTN_WIDTH), ("k", KV_WIDTH), ("v", KV_WIDTH), ("az", ATTN_WIDTH),
               ("iq", IDX_WIDTH), ("ik", IDX_DIM), ("iw", IDX_HEADS), ("gs", D_MODEL), ("ga", D_MODEL)):
    _SRC[_n] = (_o, _w)
    _o += _w
N_IN = _o

_DST = {}
_o = 0
for _n, _w in (("z", 4096), ("x", 4096), ("q", 2048), ("az", 2048), ("gs", 2048), ("ga", 2048),
               ("b", 1024), ("c", 1024), ("iq", 1024), ("k", 512), ("v", 512),
               ("dt", 128), ("ik", 128), ("iw", 128)):
    assert _o % _w == 0
    _DST[_n] = (_o, _w)
    _o += _w
PROJ_TILE_N = 512
NP = -(-_o // PROJ_TILE_N) * PROJ_TILE_N


def _col_block(name):
    off, w = _DST[name]
    return off // w


def _pick_tile(n, target, mult):
    best = None
    for t in range(mult, min(n, target) + 1, mult):
        if n % t == 0:
            best = t
    assert best is not None, (n, target, mult)
    return best


def _silu(v):
    return v / (1.0 + jnp.exp(-v))


def _sigmoid(v):
    return 1.0 / (1.0 + jnp.exp(-v))


def _dot(a, b):
    return jnp.dot(a.astype(MM), b.astype(MM), preferred_element_type=jnp.float32)


def _dot_nt(a, b):
    return lax.dot_general(a.astype(MM), b.astype(MM), (((1,), (1,)), ((), ())),
                           preferred_element_type=jnp.float32)


def _dot_tn(a, b):
    return lax.dot_general(a.astype(MM), b.astype(MM), (((0,), (0,)), ((), ())),
                           preferred_element_type=jnp.float32)


def _params(sem):
    return pltpu.CompilerParams(dimension_semantics=sem, vmem_limit_bytes=VMEM_LIMIT)


def _inproj_kernel(l_ref, h_ref, nw_ref, w_ref, o_ref, hn_ref):
    del l_ref

    @pl.when(pl.program_id(1) == 0)
    def _():
        x = h_ref[...]
        ms = jnp.mean(x * x, axis=-1, keepdims=True)
        hn_ref[...] = (x * lax.rsqrt(ms + EPS) * nw_ref[...]).astype(hn_ref.dtype)

    o_ref[...] = jnp.dot(hn_ref[...], w_ref[...], preferred_element_type=jnp.float32)


def _inproj(l_arr, h2, nw, wp):
    m, d = h2.shape
    tm = _pick_tile(m, 1056, 8)
    tn = PROJ_TILE_N
    return pl.pallas_call(
        _inproj_kernel,
        grid_spec=pltpu.PrefetchScalarGridSpec(
            num_scalar_prefetch=1, grid=(m // tm, NP // tn),
            in_specs=[pl.BlockSpec((tm, d), lambda i, j, l: (i, 0)),
                      pl.BlockSpec((1, d), lambda i, j, l: (0, 0)),
                      pl.BlockSpec((None, d, tn), lambda i, j, l: (l[0], 0, j))],
            out_specs=pl.BlockSpec((tm, tn), lambda i, j, l: (i, j)),
            scratch_shapes=[pltpu.VMEM((tm, d), MM)]),
        out_shape=jax.ShapeDtypeStruct((m, NP), jnp.float32),
        compiler_params=_params(("parallel", "arbitrary")),
        name="inproj",
    )(l_arr, h2, nw, wp)


def _split3(a):
    hi = a.astype(jnp.bfloat16)
    r1 = a - hi.astype(jnp.float32)
    mid = r1.astype(jnp.bfloat16)
    lo = (r1 - mid.astype(jnp.float32)).astype(jnp.bfloat16)
    return hi, mid, lo


def _expand_heads(a, e):
    hi, mid, lo = _split3(a)
    f = functools.partial(jnp.dot, preferred_element_type=jnp.float32)
    return f(hi, e) + f(mid, e) + f(lo, e)


def _ssd_kernel(lead, z_ref, x_ref, b_ref, c_ref, dt_ref, cwx_ref, cwb_ref, cwc_ref,
                cbx_ref, cbb_ref, cbc_ref, dtb_ref, alog_ref, dsk_ref, nw_ref,
                o_ref, extx, extb, extc, state, ybuf):
    f32 = jnp.float32
    ci = pl.program_id(1)
    q = ROW_BLOCK

    @pl.when(ci == 0)
    def _():
        extx[0:8, :] = jnp.zeros((8, D_INNER), f32)
        extb[0:8, :] = jnp.zeros((8, BC_WIDTH), f32)
        extc[0:8, :] = jnp.zeros((8, BC_WIDTH), f32)
        state[...] = jnp.zeros(state.shape, f32)

    def conv_silu(u_ref, ext, w_ref, bias_ref):
        ext[8:8 + q, :] = u_ref[...]
        acc = bias_ref[...] + w_ref[CONV_WIDTH - 1:CONV_WIDTH, :] * ext[8:8 + q, :]
        for k in range(CONV_WIDTH - 1):
            s = 8 - (CONV_WIDTH - 1) + k
            acc = acc + w_ref[k:k + 1, :] * ext[s:s + q, :]
        ext[0:8, :] = ext[q:q + 8, :]
        return _silu(acc)

    rows = ci * q + lax.broadcasted_iota(jnp.int32, (q, 1), 0)
    valid = rows >= lead

    xa = jnp.where(valid, conv_silu(x_ref, extx, cwx_ref, cbx_ref), 0.0)
    ba = conv_silu(b_ref, extb, cwb_ref, cbb_ref)
    ca = conv_silu(c_ref, extc, cwc_ref, cbc_ref)

    dtr = dt_ref[...] + dtb_ref[...]
    dt = jnp.maximum(dtr, 0.0) + jnp.log(1.0 + jnp.exp(-jnp.abs(dtr)))
    dt = jnp.where(valid, dt, 0.0)
    a = -jnp.exp(alog_ref[...])
    ri = lax.broadcasted_iota(jnp.int32, (q, q), 0)
    cj = lax.broadcasted_iota(jnp.int32, (q, q), 1)
    causal = ri >= cj
    tri = causal.astype(f32)
    acs = jnp.dot(tri, dt * a, preferred_element_type=f32, precision=lax.Precision.HIGHEST)
    acs_t = acs.T
    eacs = jnp.exp(acs)
    dte = jnp.exp(acs[q - 1:q, :] - acs)

    hrow = lax.broadcasted_iota(jnp.int32, (LANES, D_INNER), 0)
    hcol = lax.broadcasted_iota(jnp.int32, (LANES, D_INNER), 1) // SSM_HEAD_DIM
    e = (hrow == hcol).astype(jnp.bfloat16)
    dt_x = _expand_heads(dt, e)
    eacs_x = _expand_heads(eacs, e)
    dte_x = _expand_heads(dte, e)

    xdt = xa * dt_x
    xd_end = (xdt * dte_x).astype(MM)
    xdt_m = xdt.astype(MM)
    lane = lax.broadcasted_iota(jnp.int32, (q, LANES), 1)
    lo_half = lane < SSM_HEAD_DIM

    heads_per_group = SSM_HEADS // SSM_GROUPS
    for g in range(SSM_GROUPS):
        gs = slice(g * GROUP_WIDTH, (g + 1) * GROUP_WIDTH)
        bg = ba[:, g * SSM_STATE:(g + 1) * SSM_STATE]
        cg = ca[:, g * SSM_STATE:(g + 1) * SSM_STATE]
        cb = _dot_nt(cg, bg)
        h_in = state[g]
        y_off = _dot(cg, h_in) * eacs_x[:, gs]
        s_new = _dot_tn(bg, xd_end[:, gs])
        state[g] = h_in * eacs_x[q - 1:q, gs] + s_new
        for pr in range(heads_per_group // 2):
            h0 = g * heads_per_group + 2 * pr
            ms = []
            for hh in (h0, h0 + 1):
                diff = acs[:, hh:hh + 1] - acs_t[hh:hh + 1, :]
                lmat = jnp.exp(jnp.where(causal, diff, -jnp.inf))
                ms.append((cb * lmat).astype(MM))
            lhs = jnp.concatenate(ms, axis=1)
            xp = xdt_m[:, h0 * SSM_HEAD_DIM:(h0 + 2) * SSM_HEAD_DIM]
            zero = jnp.zeros_like(xp)
            rhs = jnp.concatenate([jnp.where(lo_half, xp, zero), jnp.where(lo_half, zero, xp)], axis=0)
            yd = jnp.dot(lhs, rhs, preferred_element_type=f32)
            cs = slice(h0 * SSM_HEAD_DIM, (h0 + 2) * SSM_HEAD_DIM)
            ybuf[:, cs] = yd + y_off[:, 2 * pr * SSM_HEAD_DIM:(2 * pr + 2) * SSM_HEAD_DIM]

    y = ybuf[...] + dsk_ref[...] * xa
    gz = y * _silu(z_ref[...])
    for g in range(SSM_GROUPS):
        gs = slice(g * GROUP_WIDTH, (g + 1) * GROUP_WIDTH)
        gg = gz[:, gs]
        ms = jnp.mean(gg * gg, axis=-1, keepdims=True)
        o_ref[:, gs] = (gg * lax.rsqrt(ms + EPS) * nw_ref[:, gs]).astype(o_ref.dtype)


def _ssd(proj, lead, bsz, tp, cwx, cwb, cwc, cbx, cbb, cbc, dtb, alog, dsk, nw):
    nc = tp // ROW_BLOCK
    q = ROW_BLOCK

    def seg(name):
        w = _DST[name][1]
        cbk = _col_block(name)
        return pl.BlockSpec((q, w), lambda b, c: (b * nc + c, cbk))

    def full(a):
        return pl.BlockSpec(a.shape, lambda b, c: (0, 0))

    small = (cwx, cwb, cwc, cbx, cbb, cbc, dtb, alog, dsk, nw)
    return pl.pallas_call(
        functools.partial(_ssd_kernel, lead),
        grid=(bsz, nc),
        in_specs=[seg("z"), seg("x"), seg("b"), seg("c"), seg("dt")] + [full(a) for a in small],
        out_specs=pl.BlockSpec((q, D_INNER), lambda b, c: (b * nc + c, 0)),
        out_shape=jax.ShapeDtypeStruct((bsz * tp, D_INNER), MM),
        scratch_shapes=[pltpu.VMEM((q + 8, D_INNER), jnp.float32),
                        pltpu.VMEM((q + 8, BC_WIDTH), jnp.float32),
                        pltpu.VMEM((q + 8, BC_WIDTH), jnp.float32),
                        pltpu.VMEM((SSM_GROUPS, SSM_STATE, GROUP_WIDTH), jnp.float32),
                        pltpu.VMEM((q, D_INNER), jnp.float32)],
        compiler_params=_params(("parallel", "arbitrary")),
        name="ssd",
    )(proj, proj, proj, proj, proj, *small)


def _rope128(x, tab, half):
    cos = tab[:, 0:LANES]
    s_a = tab[:, LANES:2 * LANES]
    s_b = tab[:, 2 * LANES:3 * LANES]
    return x * cos + pltpu.roll(x, LANES - half, 1) * s_a + pltpu.roll(x, half, 1) * s_b


def _rms128(x, w, width):
    ms = jnp.sum(x * x, axis=-1, keepdims=True) * (1.0 / width)
    return x * lax.rsqrt(ms + EPS) * w


def _kprep_kernel(k_ref, v_ref, ik_ref, ta_ref, ti_ref, knw_ref, iknw_ref,
                  ko_ref, vo_ref, iklo_ref, ikhi_ref):
    ta = ta_ref[...]
    for h in range(ATTN_KV_HEADS):
        cs = slice(h * ATTN_HEAD_DIM, (h + 1) * ATTN_HEAD_DIM)
        kn = _rms128(k_ref[:, cs], knw_ref[...], ATTN_HEAD_DIM)
        ko_ref[:, cs] = _rope128(kn, ta, ATTN_HEAD_DIM // ROPE_FRACTION // 2).astype(ko_ref.dtype)
    vo_ref[...] = v_ref[...].astype(vo_ref.dtype)
    ikn = _rms128(ik_ref[...], iknw_ref[...], IDX_DIM)
    ikn = _rope128(ikn, ti_ref[...], IDX_DIM // ROPE_FRACTION // 2)
    iklo_ref[...] = ikn.astype(iklo_ref.dtype)
    ikhi_ref[...] = pltpu.roll(ikn, IDX_DIM, 1).astype(ikhi_ref.dtype)


def _kprep(proj, tab_a, tab_i, knw, iknw, tp):
    m = proj.shape[0]
    tr = _pick_tile(tp, 1408, 8)
    nt = tp // tr

    def seg(name):
        w = _DST[name][1]
        cbk = _col_block(name)
        return pl.BlockSpec((tr, w), lambda i: (i, cbk))

    tab = pl.BlockSpec((tr, 3 * LANES), lambda i: (i % nt, 0))
    vec = pl.BlockSpec((1, LANES), lambda i: (0, 0))
    row = lambda w: pl.BlockSpec((tr, w), lambda i: (i, 0))
    return pl.pallas_call(
        _kprep_kernel,
        grid=(m // tr,),
        in_specs=[seg("k"), seg("v"), seg("ik"), tab, tab, vec, vec],
        out_specs=[row(KV_WIDTH), row(KV_WIDTH), row(LANES), row(LANES)],
        out_shape=[jax.ShapeDtypeStruct((m, KV_WIDTH), MM), jax.ShapeDtypeStruct((m, KV_WIDTH), MM),
                   jax.ShapeDtypeStruct((m, LANES), MM), jax.ShapeDtypeStruct((m, LANES), MM)],
        compiler_params=_params(("parallel",)),
        name="kprep",
    )(proj, proj, proj, tab_a, tab_i, knw, iknw)


def _attn_kernel(lead, k_sel, tk, q_ref, iq_ref, iw_ref, az_ref, ta_ref, ti_ref, qnw_ref,
                 kn_ref, vb_ref, iklo_ref, ikhi_ref, o_ref, key_ref, mb_ref):
    f32 = jnp.float32
    tq = ROW_BLOCK
    qi = pl.program_id(1)
    nch = (qi * tq + tq + tk - 1) // tk
    ta = ta_ref[...]
    ti = ti_ref[...]

    scale = ATTN_HEAD_DIM ** -0.5
    qh = []
    for h in range(ATTN_HEADS):
        cs = slice(h * ATTN_HEAD_DIM, (h + 1) * ATTN_HEAD_DIM)
        qn = _rms128(q_ref[:, cs], qnw_ref[...], ATTN_HEAD_DIM)
        qn = _rope128(qn, ta, ATTN_HEAD_DIM // ROPE_FRACTION // 2)
        qh.append((qn * scale).astype(MM))
    iqp = []
    for p in range(IDX_HEADS // 2):
        cs = slice(p * LANES, (p + 1) * LANES)
        iqp.append(_rope128(iq_ref[:, cs], ti, IDX_DIM // ROPE_FRACTION // 2).astype(MM))
    iw = iw_ref[...] * (IDX_HEADS ** -0.5 * IDX_DIM ** -0.5)
    iwc = [iw[:, h:h + 1] for h in range(IDX_HEADS)]

    qrow = qi * tq + lax.broadcasted_iota(jnp.int32, (tq, 1), 0)
    kcol0 = lax.broadcasted_iota(jnp.int32, (1, tk), 1)

    def score_chunk(j, carry):
        off = pl.multiple_of(j * tk, tk)
        lo = iklo_ref[pl.ds(off, tk), :]
        hi = ikhi_ref[pl.ds(off, tk), :]
        sc = jnp.zeros((tq, tk), f32)
        for p in range(IDX_HEADS // 2):
            sc = sc + iwc[2 * p] * jnp.maximum(_dot_nt(iqp[p], lo), 0.0)
            sc = sc + iwc[2 * p + 1] * jnp.maximum(_dot_nt(iqp[p], hi), 0.0)
        bits = pltpu.bitcast(sc, jnp.int32)
        keys = bits ^ ((bits >> 31) & jnp.int32(0x7FFFFFFF))
        kcol = off + kcol0
        visible = (kcol <= qrow) & (kcol >= lead)
        key_ref[:, pl.ds(off, tk)] = jnp.where(visible, keys, jnp.int32(INT_MIN))
        return carry

    lax.fori_loop(0, nch, score_chunk, 0)

    def bit_step(bi, t):
        bit = lax.shift_left(jnp.int32(1), 31 - bi)
        cand = t | bit
        cand_s = cand ^ jnp.int32(INT_MIN)

        def count_chunk(j, acc):
            off = pl.multiple_of(j * tk, tk)
            kk = key_ref[:, pl.ds(off, tk)]
            ge = jnp.where(kk >= cand_s, 1, 0)
            for c in range(tk // LANES):
                acc = acc + ge[:, c * LANES:(c + 1) * LANES]
            return acc

        acc = lax.fori_loop(0, nch, count_chunk, jnp.zeros((tq, LANES), jnp.int32))
        cnt = jnp.sum(acc, axis=-1, keepdims=True)
        return jnp.where(cnt >= k_sel, cand, t)

    t_u = lax.fori_loop(0, 32, bit_step, jnp.zeros((tq, 1), jnp.int32))
    thr = jnp.maximum(t_u ^ jnp.int32(INT_MIN), jnp.int32(INT_MIN + 1))

    def bias_chunk(j, carry):
        off = pl.multiple_of(j * tk, tk)
        kk = key_ref[:, pl.ds(off, tk)]
        mb_ref[:, pl.ds(off, tk)] = jnp.where(kk >= thr, 0.0, NEG_BIG)
        return carry

    lax.fori_loop(0, nch, bias_chunk, 0)

    for g in range(ATTN_KV_HEADS):
        q4 = jnp.concatenate(qh[g * KV_GROUP:(g + 1) * KV_GROUP], axis=0)
        cs = slice(g * ATTN_HEAD_DIM, (g + 1) * ATTN_HEAD_DIM)

        def att_chunk(j, carry, q4=q4, cs=cs):
            m, l, acc = carry
            off = pl.multiple_of(j * tk, tk)
            kc = kn_ref[pl.ds(off, tk), cs]
            vc = vb_ref[pl.ds(off, tk), cs]
            s = _dot_nt(q4, kc)
            s = (s.reshape(KV_GROUP, tq, tk) + mb_ref[:, pl.ds(off, tk)][None]).reshape(KV_GROUP * tq, tk)
            m_new = jnp.maximum(m, jnp.max(s, axis=-1, keepdims=True))
            p = jnp.exp(s - m_new)
            alpha = jnp.exp(m - m_new)
            l = alpha * l + jnp.sum(p, axis=-1, keepdims=True)
            acc = alpha * acc + jnp.dot(p.astype(MM), vc, preferred_element_type=f32)
            return m_new, l, acc

        init = (jnp.full((KV_GROUP * tq, 1), NEG_BIG, f32), jnp.zeros((KV_GROUP * tq, 1), f32),
                jnp.zeros((KV_GROUP * tq, ATTN_HEAD_DIM), f32))
        _, l, acc = lax.fori_loop(0, nch, att_chunk, init)
        o = acc / l
        for r in range(KV_GROUP):
            h = g * KV_GROUP + r
            hs = slice(h * ATTN_HEAD_DIM, (h + 1) * ATTN_HEAD_DIM)
            o_ref[:, hs] = (o[r * tq:(r + 1) * tq, :] * _silu(az_ref[:, hs])).astype(o_ref.dtype)


def _attn(proj, kn, vb, iklo, ikhi, tab_a, tab_i, qnw, lead, k_sel, bsz, tp):
    tq = ROW_BLOCK
    nq = tp // tq
    tk = ROW_BLOCK * _pick_tile(nq, 3, 1)

    def seg(name):
        w = _DST[name][1]
        cbk = _col_block(name)
        return pl.BlockSpec((tq, w), lambda b, i: (b * nq + i, cbk))

    tab = pl.BlockSpec((tq, 3 * LANES), lambda b, i: (i, 0))
    vec = pl.BlockSpec((1, LANES), lambda b, i: (0, 0))
    keys = lambda w: pl.BlockSpec((tp, w), lambda b, i: (b, 0))
    return pl.pallas_call(
        functools.partial(_attn_kernel, lead, k_sel, tk),
        grid=(bsz, nq),
        in_specs=[seg("q"), seg("iq"), seg("iw"), seg("az"), tab, tab, vec,
                  keys(KV_WIDTH), keys(KV_WIDTH), keys(LANES), keys(LANES)],
        out_specs=pl.BlockSpec((tq, ATTN_WIDTH), lambda b, i: (b * nq + i, 0)),
        out_shape=jax.ShapeDtypeStruct((bsz * tp, ATTN_WIDTH), MM),
        scratch_shapes=[pltpu.VMEM((tq, tp), jnp.int32), pltpu.VMEM((tq, tp), jnp.float32)],
        compiler_params=_params(("parallel", "arbitrary")),
        name="attn",
    )(proj, proj, proj, proj, tab_a, tab_i, qnw, kn, vb, iklo, ikhi)


def _merge_kernel(l_ref, ys_ref, ya_ref, gs_ref, ga_ref, ws_ref, wa_ref, o_ref):
    del l_ref
    ys = jnp.dot(ys_ref[...], ws_ref[...], preferred_element_type=jnp.float32)
    ya = jnp.dot(ya_ref[...], wa_ref[...], preferred_element_type=jnp.float32)
    o_ref[...] = (_sigmoid(gs_ref[...]) * ys + _sigmoid(ga_ref[...]) * ya).astype(o_ref.dtype)


def _merge(l_arr, ys, ya, proj, ws, wa):
    m = ys.shape[0]
    tm = _pick_tile(m, 1056, 8)
    tn = 512
    gsb = _DST["gs"][0] // tn
    gab = _DST["ga"][0] // tn
    return pl.pallas_call(
        _merge_kernel,
        grid_spec=pltpu.PrefetchScalarGridSpec(
            num_scalar_prefetch=1, grid=(m // tm, D_MODEL // tn),
            in_specs=[pl.BlockSpec((tm, D_INNER), lambda i, j, l: (i, 0)),
                      pl.BlockSpec((tm, ATTN_WIDTH), lambda i, j, l: (i, 0)),
                      pl.BlockSpec((tm, tn), lambda i, j, l: (i, gsb + j)),
                      pl.BlockSpec((tm, tn), lambda i, j, l: (i, gab + j)),
                      pl.BlockSpec((None, D_INNER, tn), lambda i, j, l: (l[0], 0, j)),
                      pl.BlockSpec((None, ATTN_WIDTH, tn), lambda i, j, l: (l[0], 0, j))],
            out_specs=pl.BlockSpec((tm, tn), lambda i, j, l: (i, j))),
        out_shape=jax.ShapeDtypeStruct((m, D_MODEL), MM),
        compiler_params=_params(("parallel", "arbitrary")),
        name="merge",
    )(l_arr, ys, ya, proj, proj, ws, wa)


def _outproj_kernel(lead, tp, l_ref, mg_ref, h_ref, w_ref, o_ref):
    del l_ref
    tm = mg_ref.shape[0]
    upd = jnp.dot(mg_ref[...], w_ref[...], preferred_element_type=jnp.float32)
    rows = pl.program_id(0) * tm + lax.broadcasted_iota(jnp.int32, (tm, 1), 0)
    o_ref[...] = jnp.where(rows % tp >= lead, h_ref[...] + upd, 0.0)


def _outproj(l_arr, mg, h2, wo, lead, tp):
    m = mg.shape[0]
    tm = _pick_tile(m, 1056, 8)
    tn = 512
    return pl.pallas_call(
        functools.partial(_outproj_kernel, lead, tp),
        grid_spec=pltpu.PrefetchScalarGridSpec(
            num_scalar_prefetch=1, grid=(m // tm, D_MODEL // tn),
            in_specs=[pl.BlockSpec((tm, D_MODEL), lambda i, j, l: (i, 0)),
                      pl.BlockSpec((tm, tn), lambda i, j, l: (i, j)),
                      pl.BlockSpec((None, D_MODEL, tn), lambda i, j, l: (l[0], 0, j))],
            out_specs=pl.BlockSpec((tm, tn), lambda i, j, l: (i, j))),
        out_shape=jax.ShapeDtypeStruct((m, D_MODEL), jnp.float32),
        compiler_params=_params(("parallel", "arbitrary")),
        name="outproj",
    )(l_arr, mg, h2, wo)


def _rope_table(pos, head_dim):
    rot = head_dim // ROPE_FRACTION
    half = rot // 2
    inv = ROPE_THETA ** (-jnp.arange(0, rot, 2, dtype=jnp.float32) / rot)
    ang = pos.astype(jnp.float32)[:, None] * inv[None, :]
    cos, sin = jnp.cos(ang), jnp.sin(ang)
    n = pos.shape[0]
    rest = head_dim - rot
    c = jnp.concatenate([cos, cos, jnp.ones((n, rest), jnp.float32)], axis=1)
    s_a = jnp.concatenate([-sin, jnp.zeros((n, head_dim - half), jnp.float32)], axis=1)
    s_b = jnp.concatenate([jnp.zeros((n, half), jnp.float32), sin, jnp.zeros((n, rest), jnp.float32)], axis=1)
    rep = LANES // head_dim
    return jnp.concatenate([jnp.tile(c, (1, rep)), jnp.tile(s_a, (1, rep)), jnp.tile(s_b, (1, rep))], axis=1)


def _pad_lanes(a, width=LANES):
    return jnp.pad(a, [(0, 0)] * (a.ndim - 1) + [(0, width - a.shape[-1])])


def _prep_w_in(w_in):
    depth, d, _ = w_in.shape
    parts = []
    for name, (_, w) in _DST.items():
        s0, sw = _SRC[name]
        parts.append(_pad_lanes(w_in[:, :, s0:s0 + sw], w))
    used = sum(w for _, w in _DST.values())
    parts.append(jnp.zeros((depth, d, NP - used), w_in.dtype))
    return jnp.concatenate(parts, axis=-1).astype(MM)


def kernel(x, meta_tokens, norm_w, w_in, conv_w, conv_b, dt_bias, a_log, d_skip, ssm_norm_w,
           w_ssm_out, q_norm_w, k_norm_w, idx_k_norm_w, w_attn_out, w_out):
    bsz, seq, d = x.shape
    depth = norm_w.shape[0]
    assert d == D_MODEL and w_in.shape[-1] == N_IN
    t = seq + N_META
    lead = (-t) % ROW_BLOCK
    tp = t + lead
    k_sel = min(TOPK_MAX, seq // 4)

    meta = jnp.broadcast_to(meta_tokens[None].astype(x.dtype), (bsz, N_META, d))
    h = jnp.concatenate([jnp.zeros((bsz, lead, d), x.dtype), meta, x], axis=1).reshape(bsz * tp, d)

    pos = jnp.arange(tp) - lead
    tab_a = _rope_table(pos, ATTN_HEAD_DIM)
    tab_i = _rope_table(pos, IDX_DIM)

    wp = _prep_w_in(w_in)
    ws = w_ssm_out.astype(MM)
    wa = w_attn_out.astype(MM)
    wo = w_out.astype(MM)
    xo = _SRC["x"][0] - _SRC["x"][0]
    cw_x, cw_b, cw_c = (conv_w[:, :, xo:D_INNER], conv_w[:, :, D_INNER:D_INNER + BC_WIDTH],
                        conv_w[:, :, D_INNER + BC_WIDTH:])
    cb_x, cb_b, cb_c = (conv_b[:, None, :D_INNER], conv_b[:, None, D_INNER:D_INNER + BC_WIDTH],
                        conv_b[:, None, D_INNER + BC_WIDTH:])
    dtb = _pad_lanes(dt_bias)[:, None, :]
    alog = _pad_lanes(a_log)[:, None, :]
    dsk = jnp.repeat(d_skip, SSM_HEAD_DIM, axis=-1)[:, None, :]
    snw = ssm_norm_w[:, None, :]
    iknw = _pad_lanes(idx_k_norm_w)[:, None, :]
    qnw = q_norm_w[:, None, :]
    knw = k_norm_w[:, None, :]
    nw = norm_w[:, None, :]

    def layer(l, h):
        l_arr = jnp.reshape(l, (1,)).astype(jnp.int32)
        proj = _inproj(l_arr, h, nw[l], wp)
        ys = _ssd(proj, lead, bsz, tp, cw_x[l], cw_b[l], cw_c[l], cb_x[l], cb_b[l], cb_c[l],
                  dtb[l], alog[l], dsk[l], snw[l])
        kn, vb, iklo, ikhi = _kprep(proj, tab_a, tab_i, knw[l], iknw[l], tp)
        ya = _attn(proj, kn, vb, iklo, ikhi, tab_a, tab_i, qnw[l], lead, k_sel, bsz, tp)
        mg = _merge(l_arr, ys, ya, proj, ws, wa)
        return _outproj(l_arr, mg, h, wo, lead, tp)

    h = lax.fori_loop(0, depth, layer, h)
    return h.reshape(bsz, tp, d)[:, lead + N_META:]
```

```python
import functools

import jax
import jax.numpy as jnp
from jax import lax
from jax.experimental import pallas as pl
from jax.experimental.pallas import tpu as pltpu

D_MODEL = 2048
N_META = 16
D_INNER = 2 * D_MODEL
SSM_HEAD_DIM = 64
SSM_HEADS = D_INNER // SSM_HEAD_DIM
SSM_GROUPS = 8
SSM_STATE = 128
CONV_WIDTH = 4
BC_WIDTH = SSM_GROUPS * SSM_STATE
GROUP_WIDTH = D_INNER // SSM_GROUPS
ATTN_HEADS = 16
ATTN_KV_HEADS = 4
ATTN_HEAD_DIM = 128
ATTN_WIDTH = ATTN_HEADS * ATTN_HEAD_DIM
KV_WIDTH = ATTN_KV_HEADS * ATTN_HEAD_DIM
KV_GROUP = ATTN_HEADS // ATTN_KV_HEADS
IDX_HEADS = 16
IDX_DIM = 64
IDX_WIDTH = IDX_HEADS * IDX_DIM
TOPK_MAX = 256
ROPE_THETA = 500000.0
ROPE_FRACTION = 4
EPS = 1e-6

LANES = 128
ROW_BLOCK = 128
ATTN_WIDE_CHUNKS = 4
VMEM_LIMIT = 56 * 1024 * 1024

MM = jnp.bfloat16
NEG_BIG = -1e30
LOG2E = 1.4426950408889634
INT_MIN = -2 ** 31

_SRC = {}
_o = 0
for _n, _w in (("z", D_INNER), ("x", D_INNER), ("b", BC_WIDTH), ("c", BC_WIDTH), ("dt", SSM_HEADS),
               ("q", ATTN_WIDTH), ("k", KV_WIDTH), ("v", KV_WIDTH), ("az", ATTN_WIDTH),
               ("iq", IDX_WIDTH), ("ik", IDX_DIM), ("iw", IDX_HEADS), ("gs", D_MODEL), ("ga", D_MODEL)):
    _SRC[_n] = (_o, _w)
    _o += _w
N_IN = _o

_DST = {}
_o = 0
for _n, _w in (("z", 4096), ("x", 4096), ("q", 2048), ("az", 2048), ("gs", 2048), ("ga", 2048),
               ("b", 1024), ("c", 1024), ("iq", 1024), ("k", 512), ("v", 512),
               ("dt", 128), ("ik", 128), ("iw", 128)):
    assert _o % _w == 0
    _DST[_n] = (_o, _w)
    _o += _w
PROJ_TILE_N = 512
NP = -(-_o // PROJ_TILE_N) * PROJ_TILE_N


def _col_block(name):
    off, w = _DST[name]
    return off // w


def _pick_tile(n, target, mult):
    best = None
    for t in range(mult, min(n, target) + 1, mult):
        if n % t == 0:
            best = t
    assert best is not None, (n, target, mult)
    return best


def _silu(v):
    return v / (1.0 + jnp.exp(-v))


def _sigmoid(v):
    return 1.0 / (1.0 + jnp.exp(-v))


def _dot(a, b):
    return jnp.dot(a.astype(MM), b.astype(MM), preferred_element_type=jnp.float32)


def _dot_nt(a, b):
    return lax.dot_general(a.astype(MM), b.astype(MM), (((1,), (1,)), ((), ())),
                           preferred_element_type=jnp.float32)


def _dot_tn(a, b):
    return lax.dot_general(a.astype(MM), b.astype(MM), (((0,), (0,)), ((), ())),
                           preferred_element_type=jnp.float32)


def _params(sem):
    return pltpu.CompilerParams(dimension_semantics=sem, vmem_limit_bytes=VMEM_LIMIT)


def _inproj_kernel(l_ref, h_ref, nw_ref, w_ref, o_ref, hn_ref):
    del l_ref

    @pl.when(pl.program_id(1) == 0)
    def _():
        x = h_ref[...]
        ms = jnp.mean(x * x, axis=-1, keepdims=True)
        hn_ref[...] = (x * lax.rsqrt(ms + EPS) * nw_ref[...]).astype(hn_ref.dtype)

    o_ref[...] = jnp.dot(hn_ref[...], w_ref[...], preferred_element_type=jnp.float32)


def _inproj(l_arr, h2, nw, wp):
    m, d = h2.shape
    tm = _pick_tile(m, 1056, 8)
    tn = PROJ_TILE_N
    return pl.pallas_call(
        _inproj_kernel,
        grid_spec=pltpu.PrefetchScalarGridSpec(
            num_scalar_prefetch=1, grid=(m // tm, NP // tn),
            in_specs=[pl.BlockSpec((tm, d), lambda i, j, l: (i, 0)),
                      pl.BlockSpec((1, d), lambda i, j, l: (0, 0)),
                      pl.BlockSpec((None, d, tn), lambda i, j, l: (l[0], 0, j))],
            out_specs=pl.BlockSpec((tm, tn), lambda i, j, l: (i, j)),
            scratch_shapes=[pltpu.VMEM((tm, d), MM)]),
        out_shape=jax.ShapeDtypeStruct((m, NP), jnp.float32),
        compiler_params=_params(("parallel", "arbitrary")),
        name="inproj",
    )(l_arr, h2, nw, wp)


def _split3(a):
    hi = a.astype(jnp.bfloat16)
    r1 = a - hi.astype(jnp.float32)
    mid = r1.astype(jnp.bfloat16)
    lo = (r1 - mid.astype(jnp.float32)).astype(jnp.bfloat16)
    return hi, mid, lo


def _expand_heads(a, e):
    hi, mid, lo = _split3(a)
    f = functools.partial(jnp.dot, preferred_element_type=jnp.float32)
    return f(hi, e) + f(mid, e) + f(lo, e)


def _ssd_kernel(lead, z_ref, x_ref, b_ref, c_ref, dt_ref, cwx_ref, cwb_ref, cwc_ref,
                cbx_ref, cbb_ref, cbc_ref, dtb_ref, alog_ref, dsk_ref, nw_ref,
                o_ref, extx, extb, extc, state, ybuf):
    f32 = jnp.float32
    ci = pl.program_id(1)
    q = ROW_BLOCK

    @pl.when(ci == 0)
    def _():
        extx[0:8, :] = jnp.zeros((8, D_INNER), f32)
        extb[0:8, :] = jnp.zeros((8, BC_WIDTH), f32)
        extc[0:8, :] = jnp.zeros((8, BC_WIDTH), f32)
        state[...] = jnp.zeros(state.shape, f32)

    def conv_silu(u_ref, ext, w_ref, bias_ref):
        ext[8:8 + q, :] = u_ref[...]
        acc = bias_ref[...] + w_ref[CONV_WIDTH - 1:CONV_WIDTH, :] * ext[8:8 + q, :]
        for k in range(CONV_WIDTH - 1):
            s = 8 - (CONV_WIDTH - 1) + k
            acc = acc + w_ref[k:k + 1, :] * ext[s:s + q, :]
        ext[0:8, :] = ext[q:q + 8, :]
        return _silu(acc)

    rows = ci * q + lax.broadcasted_iota(jnp.int32, (q, 1), 0)
    valid = rows >= lead

    xa = jnp.where(valid, conv_silu(x_ref, extx, cwx_ref, cbx_ref), 0.0)
    ba = conv_silu(b_ref, extb, cwb_ref, cbb_ref)
    ca = conv_silu(c_ref, extc, cwc_ref, cbc_ref)

    dtr = dt_ref[...] + dtb_ref[...]
    dt = jnp.maximum(dtr, 0.0) + jnp.log(1.0 + jnp.exp(-jnp.abs(dtr)))
    dt = jnp.where(valid, dt, 0.0)
    a = -jnp.exp(alog_ref[...])
    ri = lax.broadcasted_iota(jnp.int32, (q, q), 0)
    cj = lax.broadcasted_iota(jnp.int32, (q, q), 1)
    causal = ri >= cj
    tri = causal.astype(f32)
    acs = jnp.dot(tri, dt * a, preferred_element_type=f32, precision=lax.Precision.HIGHEST)
    acs_t = acs.T
    eacs = jnp.exp(acs)
    dte = jnp.exp(acs[q - 1:q, :] - acs)

    hrow = lax.broadcasted_iota(jnp.int32, (LANES, D_INNER), 0)
    hcol = lax.broadcasted_iota(jnp.int32, (LANES, D_INNER), 1) // SSM_HEAD_DIM
    e = (hrow == hcol).astype(jnp.bfloat16)
    dt_x = _expand_heads(dt, e)
    eacs_x = _expand_heads(eacs, e)
    dte_x = _expand_heads(dte, e)

    xdt = xa * dt_x
    xd_end = (xdt * dte_x).astype(MM)
    xdt_m = xdt.astype(MM)
    lane = lax.broadcasted_iota(jnp.int32, (q, LANES), 1)
    lo_half = lane < SSM_HEAD_DIM

    heads_per_group = SSM_HEADS // SSM_GROUPS
    for g in range(SSM_GROUPS):
        gs = slice(g * GROUP_WIDTH, (g + 1) * GROUP_WIDTH)
        bg = ba[:, g * SSM_STATE:(g + 1) * SSM_STATE]
        cg = ca[:, g * SSM_STATE:(g + 1) * SSM_STATE]
        cb = _dot_nt(cg, bg)
        h_in = state[g]
        y_off = _dot(cg, h_in) * eacs_x[:, gs]
        s_new = _dot_tn(bg, xd_end[:, gs])
        state[g] = h_in * eacs_x[q - 1:q, gs] + s_new
        for pr in range(heads_per_group // 2):
            h0 = g * heads_per_group + 2 * pr
            ms = []
            for hh in (h0, h0 + 1):
                diff = acs[:, hh:hh + 1] - acs_t[hh:hh + 1, :]
                lmat = jnp.exp(jnp.where(causal, diff, -jnp.inf))
                ms.append((cb * lmat).astype(MM))
            lhs = jnp.concatenate(ms, axis=1)
            xp = xdt_m[:, h0 * SSM_HEAD_DIM:(h0 + 2) * SSM_HEAD_DIM]
            zero = jnp.zeros_like(xp)
            rhs = jnp.concatenate([jnp.where(lo_half, xp, zero), jnp.where(lo_half, zero, xp)], axis=0)
            yd = jnp.dot(lhs, rhs, preferred_element_type=f32)
            cs = slice(h0 * SSM_HEAD_DIM, (h0 + 2) * SSM_HEAD_DIM)
            ybuf[:, cs] = yd + y_off[:, 2 * pr * SSM_HEAD_DIM:(2 * pr + 2) * SSM_HEAD_DIM]

    y = ybuf[...] + dsk_ref[...] * xa
    gz = y * _silu(z_ref[...])
    for g in range(SSM_GROUPS):
        gs = slice(g * GROUP_WIDTH, (g + 1) * GROUP_WIDTH)
        gg = gz[:, gs]
        ms = jnp.mean(gg * gg, axis=-1, keepdims=True)
        o_ref[:, gs] = (gg * lax.rsqrt(ms + EPS) * nw_ref[:, gs]).astype(o_ref.dtype)


def _ssd(proj, lead, bsz, tp, cwx, cwb, cwc, cbx, cbb, cbc, dtb, alog, dsk, nw):
    nc = tp // ROW_BLOCK
    q = ROW_BLOCK

    def seg(name):
        w = _DST[name][1]
        cbk = _col_block(name)
        return pl.BlockSpec((q, w), lambda b, c: (b * nc + c, cbk))

    def full(a):
        return pl.BlockSpec(a.shape, lambda b, c: (0, 0))

    small = (cwx, cwb, cwc, cbx, cbb, cbc, dtb, alog, dsk, nw)
    return pl.pallas_call(
        functools.partial(_ssd_kernel, lead),
        grid=(bsz, nc),
        in_specs=[seg("z"), seg("x"), seg("b"), seg("c"), seg("dt")] + [full(a) for a in small],
        out_specs=pl.BlockSpec((q, D_INNER), lambda b, c: (b * nc + c, 0)),
        out_shape=jax.ShapeDtypeStruct((bsz * tp, D_INNER), MM),
        scratch_shapes=[pltpu.VMEM((q + 8, D_INNER), jnp.float32),
                        pltpu.VMEM((q + 8, BC_WIDTH), jnp.float32),
                        pltpu.VMEM((q + 8, BC_WIDTH), jnp.float32),
                        pltpu.VMEM((SSM_GROUPS, SSM_STATE, GROUP_WIDTH), jnp.float32),
                        pltpu.VMEM((q, D_INNER), jnp.float32)],
        compiler_params=_params(("parallel", "arbitrary")),
        name="ssd",
    )(proj, proj, proj, proj, proj, *small)


def _split2(a):
    hi = a.astype(jnp.bfloat16)
    lo = (a - hi.astype(jnp.float32)).astype(jnp.bfloat16)
    return hi, lo


def _dot2(a, b):
    hi, lo = _split2(a)
    f = functools.partial(jnp.dot, preferred_element_type=jnp.float32)
    return f(hi, b) + f(lo, b)


def _rot_half_matrix(head_dim):
    half = head_dim // ROPE_FRACTION // 2
    rj = lax.broadcasted_iota(jnp.int32, (LANES, LANES), 0)
    ci = lax.broadcasted_iota(jnp.int32, (LANES, LANES), 1)
    cm = ci % head_dim
    neg = (cm < half) & (rj == ci + half)
    pos = (cm >= half) & (cm < 2 * half) & (rj == ci - half)
    return jnp.where(neg, -1.0, jnp.where(pos, 1.0, 0.0)).astype(jnp.bfloat16)


def _rope128(x, tab, half):
    cos = tab[:, 0:LANES]
    s_a = tab[:, LANES:2 * LANES]
    s_b = tab[:, 2 * LANES:3 * LANES]
    return x * cos + pltpu.roll(x, LANES - half, 1) * s_a + pltpu.roll(x, half, 1) * s_b


def _rms128(x, w, width):
    ms = jnp.sum(x * x, axis=-1, keepdims=True) * (1.0 / width)
    return x * lax.rsqrt(ms + EPS) * w


def _kprep_kernel(k_ref, v_ref, ik_ref, ta_ref, ti_ref, knw_ref, iknw_ref,
                  ko_ref, vo_ref, iklo_ref, ikhi_ref):
    ta = ta_ref[...]
    for h in range(ATTN_KV_HEADS):
        cs = slice(h * ATTN_HEAD_DIM, (h + 1) * ATTN_HEAD_DIM)
        kn = _rms128(k_ref[:, cs], knw_ref[...], ATTN_HEAD_DIM)
        ko_ref[:, cs] = _rope128(kn, ta, ATTN_HEAD_DIM // ROPE_FRACTION // 2).astype(ko_ref.dtype)
    vo_ref[...] = v_ref[...].astype(vo_ref.dtype)
    ikn = _rms128(ik_ref[...], iknw_ref[...], IDX_DIM)
    ikn = _rope128(ikn, ti_ref[...], IDX_DIM // ROPE_FRACTION // 2)
    iklo_ref[...] = ikn.astype(iklo_ref.dtype)
    ikhi_ref[...] = pltpu.roll(ikn, IDX_DIM, 1).astype(ikhi_ref.dtype)


def _kprep(proj, tab_a, tab_i, knw, iknw, tp):
    m = proj.shape[0]
    tr = _pick_tile(tp, 1408, 8)
    nt = tp // tr

    def seg(name):
        w = _DST[name][1]
        cbk = _col_block(name)
        return pl.BlockSpec((tr, w), lambda i: (i, cbk))

    tab = pl.BlockSpec((tr, 3 * LANES), lambda i: (i % nt, 0))
    vec = pl.BlockSpec((1, LANES), lambda i: (0, 0))
    row = lambda w: pl.BlockSpec((tr, w), lambda i: (i, 0))
    return pl.pallas_call(
        _kprep_kernel,
        grid=(m // tr,),
        in_specs=[seg("k"), seg("v"), seg("ik"), tab, tab, vec, vec],
        out_specs=[row(KV_WIDTH), row(KV_WIDTH), row(LANES), row(LANES)],
        out_shape=[jax.ShapeDtypeStruct((m, KV_WIDTH), MM), jax.ShapeDtypeStruct((m, KV_WIDTH), MM),
                   jax.ShapeDtypeStruct((m, LANES), MM), jax.ShapeDtypeStruct((m, LANES), MM)],
        compiler_params=_params(("parallel",)),
        name="kprep",
    )(proj, proj, proj, tab_a, tab_i, knw, iknw)


def _attn_kernel(lead, k_sel, tk, tkw, q_ref, iq_ref, iw_ref, az_ref, ta_ref, ti_ref, qnw_ref,
                 kn_ref, vb_ref, iklo_ref, ikhi_ref, o_ref, key_ref, mb_ref, iwb_ref, s_ref):
    f32 = jnp.float32
    tq = ROW_BLOCK
    nlb = tk // LANES
    qi = pl.program_id(1)
    nch = (qi * tq + tq + tk - 1) // tk
    ta = ta_ref[...]
    ti = ti_ref[...]

    ones_m = jnp.ones((LANES, LANES), jnp.bfloat16)
    scale = ATTN_HEAD_DIM ** -0.5 * LOG2E
    xs = jnp.concatenate([q_ref[:, h * ATTN_HEAD_DIM:(h + 1) * ATTN_HEAD_DIM] for h in range(ATTN_HEADS)],
                         axis=0)
    ms = _dot2(xs * xs, ones_m) * (1.0 / ATTN_HEAD_DIM)
    xn = xs * lax.rsqrt(ms + EPS) * qnw_ref[...]
    rot = _dot2(xn, _rot_half_matrix(ATTN_HEAD_DIM))
    cos_a, sin_a = ta[:, 0:LANES], ta[:, 2 * LANES:3 * LANES] - ta[:, LANES:2 * LANES]
    q_all = xn.reshape(ATTN_HEADS, tq, LANES) * cos_a[None] + rot.reshape(ATTN_HEADS, tq, LANES) * sin_a[None]
    q_all = (q_all * scale).astype(MM).reshape(ATTN_HEADS * tq, LANES)
    npair = IDX_HEADS // 2
    xi = jnp.concatenate([iq_ref[:, p * LANES:(p + 1) * LANES] for p in range(npair)], axis=0)
    roti = _dot2(xi, _rot_half_matrix(IDX_DIM))
    cos_i, sin_i = ti[:, 0:LANES], ti[:, 2 * LANES:3 * LANES] - ti[:, LANES:2 * LANES]
    iq_all = xi.reshape(npair, tq, LANES) * cos_i[None] + roti.reshape(npair, tq, LANES) * sin_i[None]
    iq_all = iq_all.astype(MM).reshape(npair * tq, LANES)
    iw = iw_ref[...] * (IDX_HEADS ** -0.5 * IDX_DIM ** -0.5)
    for h in range(IDX_HEADS):
        iwb_ref[h] = jnp.broadcast_to(iw[:, h:h + 1], (tq, LANES))

    qrow = qi * tq + lax.broadcasted_iota(jnp.int32, (tq, 1), 0)
    kcol0 = lax.broadcasted_iota(jnp.int32, (1, tk), 1)

    def score_chunk(j, carry):
        off = pl.multiple_of(j * tk, tk)
        l_even = _dot_nt(iq_all, iklo_ref[pl.ds(off, tk), :])
        l_odd = _dot_nt(iq_all, ikhi_ref[pl.ds(off, tk), :])
        cols = []
        for c in range(nlb):
            ls = slice(c * LANES, (c + 1) * LANES)
            sc = None
            for p in range(IDX_HEADS // 2):
                rs = slice(p * tq, (p + 1) * tq)
                term = (iwb_ref[2 * p] * jnp.maximum(l_even[rs, ls], 0.0)
                        + iwb_ref[2 * p + 1] * jnp.maximum(l_odd[rs, ls], 0.0))
                sc = term if sc is None else sc + term
            cols.append(sc)
        sc = jnp.concatenate(cols, axis=1)
        bits = pltpu.bitcast(sc, jnp.int32)
        keys = bits ^ ((bits >> 31) & jnp.int32(0x7FFFFFFF))
        kcol = off + kcol0
        visible = (kcol <= qrow) & (kcol >= lead)
        key_ref[:, pl.ds(off, tk)] = jnp.where(visible, keys, jnp.int32(INT_MIN))
        return carry

    lax.fori_loop(0, nch, score_chunk, 0)

    def bit_step(bi, t):
        bit = lax.shift_left(jnp.int32(1), 31 - bi)
        cand = t | bit
        cand_s = cand ^ jnp.int32(INT_MIN)

        def count_chunk(j, acc):
            off = pl.multiple_of(j * tk, tk)
            kk = key_ref[:, pl.ds(off, tk)]
            ge = jnp.where(kk >= cand_s, 1.0, 0.0)
            for c in range(nlb):
                acc = acc + ge[:, c * LANES:(c + 1) * LANES]
            return acc

        acc = lax.fori_loop(0, nch, count_chunk, jnp.zeros((tq, LANES), f32))
        cnt = jnp.sum(acc, axis=-1, keepdims=True)
        return jnp.where(cnt >= float(k_sel), cand, t)

    t_u = lax.fori_loop(0, 32, bit_step, jnp.zeros((tq, 1), jnp.int32))
    thr = jnp.maximum(t_u ^ jnp.int32(INT_MIN), jnp.int32(INT_MIN + 1))

    def bias_chunk(j, carry):
        off = pl.multiple_of(j * tk, tk)
        kk = key_ref[:, pl.ds(off, tk)]
        mb_ref[:, pl.ds(off, tk)] = jnp.where(kk >= thr, 0.0, NEG_BIG)
        return carry

    lax.fori_loop(0, nch, bias_chunk, 0)

    rows4 = KV_GROUP * tq
    q4s = [q_all[g * rows4:(g + 1) * rows4] for g in range(ATTN_KV_HEADS)]
    macc0 = jnp.full((rows4, LANES), NEG_BIG, f32)
    acc0 = jnp.zeros((rows4, 2 * ATTN_HEAD_DIM), f32)

    def logits_chunk(g, off, w, macc):
        cs = slice(g * ATTN_HEAD_DIM, (g + 1) * ATTN_HEAD_DIM)
        s = _dot_nt(q4s[g], kn_ref[pl.ds(off, w), cs])
        s = (s.reshape(KV_GROUP, tq, w) + mb_ref[:, pl.ds(off, w)][None]).reshape(rows4, w)
        s_ref[g % 2, :, pl.ds(off, w)] = s
        for c in range(w // LANES):
            macc = jnp.maximum(macc, s[:, c * LANES:(c + 1) * LANES])
        return macc

    def pv_chunk(g, m, off, w, acc):
        cs = slice(g * ATTN_HEAD_DIM, (g + 1) * ATTN_HEAD_DIM)
        p = jnp.exp2(s_ref[g % 2, :, pl.ds(off, w)] - m).astype(MM)
        v1 = jnp.concatenate([vb_ref[pl.ds(off, w), cs], jnp.ones((w, ATTN_HEAD_DIM), MM)], axis=1)
        return acc + jnp.dot(p, v1, preferred_element_type=f32)

    needed = qi * tq + tq
    nwide = needed // tkw
    nnarrow = (needed - nwide * tkw + tk - 1) // tk

    def sweep(fn, carry):
        carry = lax.fori_loop(0, nwide, lambda j, c: fn(pl.multiple_of(j * tkw, tkw), tkw, c), carry)
        return lax.fori_loop(
            0, nnarrow, lambda j, c: fn(pl.multiple_of(nwide * tkw + j * tk, tk), tk, c), carry)

    macc = sweep(functools.partial(logits_chunk, 0), macc0)
    for g in range(ATTN_KV_HEADS):
        m = jnp.max(macc, axis=-1, keepdims=True)
        if g + 1 < ATTN_KV_HEADS:
            def both(off, w, carry, g=g, m=m):
                return pv_chunk(g, m, off, w, carry[0]), logits_chunk(g + 1, off, w, carry[1])
            acc, macc = sweep(both, (acc0, macc0))
        else:
            acc = sweep(functools.partial(pv_chunk, g, m), acc0)
        o = acc[:, :ATTN_HEAD_DIM] / acc[:, ATTN_HEAD_DIM:]
        for r in range(KV_GROUP):
            h = g * KV_GROUP + r
            hs = slice(h * ATTN_HEAD_DIM, (h + 1) * ATTN_HEAD_DIM)
            o_ref[:, hs] = (o[r * tq:(r + 1) * tq, :] * _silu(az_ref[:, hs])).astype(o_ref.dtype)


def _attn(proj, kn, vb, iklo, ikhi, tab_a, tab_i, qnw, lead, k_sel, bsz, tp):
    tq = ROW_BLOCK
    nq = tp // tq
    tk = ROW_BLOCK * _pick_tile(nq, 3, 1)
    tkw = ATTN_WIDE_CHUNKS * tk

    def seg(name):
        w = _DST[name][1]
        cbk = _col_block(name)
        return pl.BlockSpec((tq, w), lambda b, i: (b * nq + i, cbk))

    tab = pl.BlockSpec((tq, 3 * LANES), lambda b, i: (i, 0))
    vec = pl.BlockSpec((1, LANES), lambda b, i: (0, 0))
    keys = lambda w: pl.BlockSpec((tp, w), lambda b, i: (b, 0), pipeline_mode=pl.Buffered(1))
    return pl.pallas_call(
        functools.partial(_attn_kernel, lead, k_sel, tk, tkw),
        grid=(bsz, nq),
        in_specs=[seg("q"), seg("iq"), seg("iw"), seg("az"), tab, tab, vec,
                  keys(KV_WIDTH), keys(KV_WIDTH), keys(LANES), keys(LANES)],
        out_specs=pl.BlockSpec((tq, ATTN_WIDTH), lambda b, i: (b * nq + i, 0)),
        out_shape=jax.ShapeDtypeStruct((bsz * tp, ATTN_WIDTH), MM),
        scratch_shapes=[pltpu.VMEM((tq, tp), jnp.int32), pltpu.VMEM((tq, tp), jnp.float32),
                        pltpu.VMEM((IDX_HEADS, tq, LANES), jnp.float32),
                        pltpu.VMEM((2, KV_GROUP * tq, tp), jnp.float32)],
        compiler_params=_params(("parallel", "arbitrary")),
        name="attn",
    )(proj, proj, proj, proj, tab_a, tab_i, qnw, kn, vb, iklo, ikhi)


def _merge_kernel(l_ref, ys_ref, ya_ref, gs_ref, ga_ref, ws_ref, wa_ref, o_ref):
    del l_ref
    ys = jnp.dot(ys_ref[...], ws_ref[...], preferred_element_type=jnp.float32)
    ya = jnp.dot(ya_ref[...], wa_ref[...], preferred_element_type=jnp.float32)
    o_ref[...] = (_sigmoid(gs_ref[...]) * ys + _sigmoid(ga_ref[...]) * ya).astype(o_ref.dtype)


def _merge(l_arr, ys, ya, proj, ws, wa):
    m = ys.shape[0]
    tm = _pick_tile(m, 1056, 8)
    tn = 512
    gsb = _DST["gs"][0] // tn
    gab = _DST["ga"][0] // tn
    return pl.pallas_call(
        _merge_kernel,
        grid_spec=pltpu.PrefetchScalarGridSpec(
            num_scalar_prefetch=1, grid=(m // tm, D_MODEL // tn),
            in_specs=[pl.BlockSpec((tm, D_INNER), lambda i, j, l: (i, 0)),
                      pl.BlockSpec((tm, ATTN_WIDTH), lambda i, j, l: (i, 0)),
                      pl.BlockSpec((tm, tn), lambda i, j, l: (i, gsb + j)),
                      pl.BlockSpec((tm, tn), lambda i, j, l: (i, gab + j)),
                      pl.BlockSpec((None, D_INNER, tn), lambda i, j, l: (l[0], 0, j)),
                      pl.BlockSpec((None, ATTN_WIDTH, tn), lambda i, j, l: (l[0], 0, j))],
            out_specs=pl.BlockSpec((tm, tn), lambda i, j, l: (i, j))),
        out_shape=jax.ShapeDtypeStruct((m, D_MODEL), MM),
        compiler_params=_params(("parallel", "arbitrary")),
        name="merge",
    )(l_arr, ys, ya, proj, proj, ws, wa)


def _outproj_kernel(lead, tp, l_ref, mg_ref, h_ref, w_ref, o_ref):
    del l_ref
    tm = mg_ref.shape[0]
    upd = jnp.dot(mg_ref[...], w_ref[...], preferred_element_type=jnp.float32)
    rows = pl.program_id(0) * tm + lax.broadcasted_iota(jnp.int32, (tm, 1), 0)
    o_ref[...] = jnp.where(rows % tp >= lead, h_ref[...] + upd, 0.0)


def _outproj(l_arr, mg, h2, wo, lead, tp):
    m = mg.shape[0]
    tm = _pick_tile(m, 1056, 8)
    tn = 512
    return pl.pallas_call(
        functools.partial(_outproj_kernel, lead, tp),
        grid_spec=pltpu.PrefetchScalarGridSpec(
            num_scalar_prefetch=1, grid=(m // tm, D_MODEL // tn),
            in_specs=[pl.BlockSpec((tm, D_MODEL), lambda i, j, l: (i, 0)),
                      pl.BlockSpec((tm, tn), lambda i, j, l: (i, j)),
                      pl.BlockSpec((None, D_MODEL, tn), lambda i, j, l: (l[0], 0, j))],
            out_specs=pl.BlockSpec((tm, tn), lambda i, j, l: (i, j))),
        out_shape=jax.ShapeDtypeStruct((m, D_MODEL), jnp.float32),
        compiler_params=_params(("parallel", "arbitrary")),
        name="outproj",
    )(l_arr, mg, h2, wo)


def _rope_table(pos, head_dim):
    rot = head_dim // ROPE_FRACTION
    half = rot // 2
    inv = ROPE_THETA ** (-jnp.arange(0, rot, 2, dtype=jnp.float32) / rot)
    ang = pos.astype(jnp.float32)[:, None] * inv[None, :]
    cos, sin = jnp.cos(ang), jnp.sin(ang)
    n = pos.shape[0]
    rest = head_dim - rot
    c = jnp.concatenate([cos, cos, jnp.ones((n, rest), jnp.float32)], axis=1)
    s_a = jnp.concatenate([-sin, jnp.zeros((n, head_dim - half), jnp.float32)], axis=1)
    s_b = jnp.concatenate([jnp.zeros((n, half), jnp.float32), sin, jnp.zeros((n, rest), jnp.float32)], axis=1)
    rep = LANES // head_dim
    return jnp.concatenate([jnp.tile(c, (1, rep)), jnp.tile(s_a, (1, rep)), jnp.tile(s_b, (1, rep))], axis=1)


def _pad_lanes(a, width=LANES):
    return jnp.pad(a, [(0, 0)] * (a.ndim - 1) + [(0, width - a.shape[-1])])


def _prep_w_in(w_in):
    depth, d, _ = w_in.shape
    parts = []
    for name, (_, w) in _DST.items():
        s0, sw = _SRC[name]
        parts.append(_pad_lanes(w_in[:, :, s0:s0 + sw], w))
    used = sum(w for _, w in _DST.values())
    parts.append(jnp.zeros((depth, d, NP - used), w_in.dtype))
    return jnp.concatenate(parts, axis=-1).astype(MM)


def kernel(x, meta_tokens, norm_w, w_in, conv_w, conv_b, dt_bias, a_log, d_skip, ssm_norm_w,
           w_ssm_out, q_norm_w, k_norm_w, idx_k_norm_w, w_attn_out, w_out):
    bsz, seq, d = x.shape
    depth = norm_w.shape[0]
    assert d == D_MODEL and w_in.shape[-1] == N_IN
    t = seq + N_META
    lead = (-t) % ROW_BLOCK
    tp = t + lead
    k_sel = min(TOPK_MAX, seq // 4)

    meta = jnp.broadcast_to(meta_tokens[None].astype(x.dtype), (bsz, N_META, d))
    h = jnp.concatenate([jnp.zeros((bsz, lead, d), x.dtype), meta, x], axis=1).reshape(bsz * tp, d)

    pos = jnp.arange(tp) - lead
    tab_a = _rope_table(pos, ATTN_HEAD_DIM)
    tab_i = _rope_table(pos, IDX_DIM)

    wp = _prep_w_in(w_in)
    ws = w_ssm_out.astype(MM)
    wa = w_attn_out.astype(MM)
    wo = w_out.astype(MM)
    cw_x, cw_b, cw_c = (conv_w[:, :, :D_INNER], conv_w[:, :, D_INNER:D_INNER + BC_WIDTH],
                        conv_w[:, :, D_INNER + BC_WIDTH:])
    cb_x, cb_b, cb_c = (conv_b[:, None, :D_INNER], conv_b[:, None, D_INNER:D_INNER + BC_WIDTH],
                        conv_b[:, None, D_INNER + BC_WIDTH:])
    dtb = _pad_lanes(dt_bias)[:, None, :]
    alog = _pad_lanes(a_log)[:, None, :]
    dsk = jnp.repeat(d_skip, SSM_HEAD_DIM, axis=-1)[:, None, :]
    snw = ssm_norm_w[:, None, :]
    iknw = _pad_lanes(idx_k_norm_w)[:, None, :]
    qnw = q_norm_w[:, None, :]
    knw = k_norm_w[:, None, :]
    nw = norm_w[:, None, :]

    def layer(l, h):
        l_arr = jnp.reshape(l, (1,)).astype(jnp.int32)
        proj = _inproj(l_arr, h, nw[l], wp)
        ys = _ssd(proj, lead, bsz, tp, cw_x[l], cw_b[l], cw_c[l], cb_x[l], cb_b[l], cb_c[l],
                  dtb[l], alog[l], dsk[l], snw[l])
        kn, vb, iklo, ikhi = _kprep(proj, tab_a, tab_i, knw[l], iknw[l], tp)
        ya = _attn(proj, kn, vb, iklo, ikhi, tab_a, tab_i, qnw[l], lead, k_sel, bsz, tp)
        mg = _merge(l_arr, ys, ya, proj, ws, wa)
        return _outproj(l_arr, mg, h, wo, lead, tp)

    h = lax.fori_loop(0, depth, layer, h)
    return h.reshape(bsz, tp, d)[:, lead + N_META:]
```

```python
import functools

import jax
import jax.numpy as jnp
from jax import lax
from jax.experimental import pallas as pl
from jax.experimental.pallas import tpu as pltpu

D_MODEL = 2048
N_META = 16
D_INNER = 2 * D_MODEL
SSM_HEAD_DIM = 64
SSM_HEADS = D_INNER // SSM_HEAD_DIM
SSM_GROUPS = 8
SSM_STATE = 128
CONV_WIDTH = 4
BC_WIDTH = SSM_GROUPS * SSM_STATE
GROUP_WIDTH = D_INNER // SSM_GROUPS
ATTN_HEADS = 16
ATTN_KV_HEADS = 4
ATTN_HEAD_DIM = 128
ATTN_WIDTH = ATTN_HEADS * ATTN_HEAD_DIM
KV_WIDTH = ATTN_KV_HEADS * ATTN_HEAD_DIM
KV_GROUP = ATTN_HEADS // ATTN_KV_HEADS
IDX_HEADS = 16
IDX_DIM = 64
IDX_WIDTH = IDX_HEADS * IDX_DIM
TOPK_MAX = 256
ROPE_THETA = 500000.0
ROPE_FRACTION = 4
EPS = 1e-6

LANES = 128
ROW_BLOCK = 128
ATTN_WIDE_CHUNKS = 4
VMEM_LIMIT = 56 * 1024 * 1024

MM = jnp.bfloat16
NEG_BIG = -1e30
LOG2E = 1.4426950408889634
INT_MIN = -2 ** 31

_SRC = {}
_o = 0
for _n, _w in (("z", D_INNER), ("x", D_INNER), ("b", BC_WIDTH), ("c", BC_WIDTH), ("dt", SSM_HEADS),
               ("q", ATTN_WIDTH), ("k", KV_WIDTH), ("v", KV_WIDTH), ("az", ATTN_WIDTH),
               ("iq", IDX_WIDTH), ("ik", IDX_DIM), ("iw", IDX_HEADS), ("gs", D_MODEL), ("ga", D_MODEL)):
    _SRC[_n] = (_o, _w)
    _o += _w
N_IN = _o

_DST = {}
_o = 0
for _n, _w in (("z", 4096), ("x", 4096), ("q", 2048), ("az", 2048), ("gs", 2048), ("ga", 2048),
               ("b", 1024), ("c", 1024), ("iq", 1024), ("k", 512), ("v", 512),
               ("dt", 128), ("ik", 128), ("iw", 128)):
    assert _o % _w == 0
    _DST[_n] = (_o, _w)
    _o += _w
PROJ_TILE_N = 512
NP = -(-_o // PROJ_TILE_N) * PROJ_TILE_N


def _col_block(name):
    off, w = _DST[name]
    return off // w


def _pick_tile(n, target, mult):
    best = None
    for t in range(mult, min(n, target) + 1, mult):
        if n % t == 0:
            best = t
    assert best is not None, (n, target, mult)
    return best


def _sigmoid(v):
    return 0.5 + 0.5 * jnp.tanh(0.5 * v)


def _silu(v):
    h = 0.5 * v
    return h + h * jnp.tanh(h)


def _dot(a, b):
    return jnp.dot(a.astype(MM), b.astype(MM), preferred_element_type=jnp.float32)


def _dot_nt(a, b):
    return lax.dot_general(a.astype(MM), b.astype(MM), (((1,), (1,)), ((), ())),
                           preferred_element_type=jnp.float32)


def _dot_tn(a, b):
    return lax.dot_general(a.astype(MM), b.astype(MM), (((0,), (0,)), ((), ())),
                           preferred_element_type=jnp.float32)


def _params(sem):
    return pltpu.CompilerParams(dimension_semantics=sem, vmem_limit_bytes=VMEM_LIMIT)


def _inproj_kernel(l_ref, h_ref, nw_ref, w_ref, o_ref, hn_ref):
    del l_ref

    @pl.when(pl.program_id(1) == 0)
    def _():
        x = h_ref[...]
        ms = jnp.mean(x * x, axis=-1, keepdims=True)
        hn_ref[...] = (x * lax.rsqrt(ms + EPS) * nw_ref[...]).astype(hn_ref.dtype)

    o_ref[...] = jnp.dot(hn_ref[...], w_ref[...], preferred_element_type=jnp.float32)


def _inproj(l_arr, h2, nw, wp):
    m, d = h2.shape
    tm = _pick_tile(m, 1056, 8)
    tn = PROJ_TILE_N
    return pl.pallas_call(
        _inproj_kernel,
        grid_spec=pltpu.PrefetchScalarGridSpec(
            num_scalar_prefetch=1, grid=(m // tm, NP // tn),
            in_specs=[pl.BlockSpec((tm, d), lambda i, j, l: (i, 0)),
                      pl.BlockSpec((1, d), lambda i, j, l: (0, 0)),
                      pl.BlockSpec((None, d, tn), lambda i, j, l: (l[0], 0, j))],
            out_specs=pl.BlockSpec((tm, tn), lambda i, j, l: (i, j)),
            scratch_shapes=[pltpu.VMEM((tm, d), MM)]),
        out_shape=jax.ShapeDtypeStruct((m, NP), jnp.float32),
        compiler_params=_params(("parallel", "arbitrary")),
        name="inproj",
    )(l_arr, h2, nw, wp)


def _split3(a):
    hi = a.astype(jnp.bfloat16)
    r1 = a - hi.astype(jnp.float32)
    mid = r1.astype(jnp.bfloat16)
    lo = (r1 - mid.astype(jnp.float32)).astype(jnp.bfloat16)
    return hi, mid, lo


def _expand_heads(a, e):
    hi, mid, lo = _split3(a)
    f = functools.partial(jnp.dot, preferred_element_type=jnp.float32)
    return f(hi, e) + f(mid, e) + f(lo, e)


def _ssd_kernel(lead, z_ref, x_ref, b_ref, c_ref, dt_ref, cwx_ref, cwb_ref, cwc_ref,
                cbx_ref, cbb_ref, cbc_ref, dtb_ref, alog_ref, dsk_ref, nw_ref,
                o_ref, extx, extb, extc, state, ybuf, e_ref):
    f32 = jnp.float32
    ci = pl.program_id(1)
    q = ROW_BLOCK

    @pl.when(ci == 0)
    def _():
        extx[0:8, :] = jnp.zeros((8, D_INNER), f32)
        extb[0:8, :] = jnp.zeros((8, BC_WIDTH), f32)
        extc[0:8, :] = jnp.zeros((8, BC_WIDTH), f32)
        state[...] = jnp.zeros(state.shape, f32)
        hrow = lax.broadcasted_iota(jnp.int32, (LANES, D_INNER), 0)
        hcol = lax.broadcasted_iota(jnp.int32, (LANES, D_INNER), 1) // SSM_HEAD_DIM
        e_ref[...] = (hrow == hcol).astype(jnp.bfloat16)

    def conv_silu(u_ref, ext, w_ref, bias_ref):
        ext[8:8 + q, :] = u_ref[...]
        acc = bias_ref[...] + w_ref[CONV_WIDTH - 1:CONV_WIDTH, :] * ext[8:8 + q, :]
        for k in range(CONV_WIDTH - 1):
            s = 8 - (CONV_WIDTH - 1) + k
            acc = acc + w_ref[k:k + 1, :] * ext[s:s + q, :]
        ext[0:8, :] = ext[q:q + 8, :]
        return _silu(acc)

    rows = ci * q + lax.broadcasted_iota(jnp.int32, (q, 1), 0)
    valid = rows >= lead

    xa = jnp.where(valid, conv_silu(x_ref, extx, cwx_ref, cbx_ref), 0.0)
    ba = conv_silu(b_ref, extb, cwb_ref, cbb_ref)
    ca = conv_silu(c_ref, extc, cwc_ref, cbc_ref)

    dtr = dt_ref[...] + dtb_ref[...]
    dt = jnp.maximum(dtr, 0.0) + jnp.log(1.0 + jnp.exp(-jnp.abs(dtr)))
    dt = jnp.where(valid, dt, 0.0)
    a = -jnp.exp(alog_ref[...])
    ri = lax.broadcasted_iota(jnp.int32, (q, q), 0)
    cj = lax.broadcasted_iota(jnp.int32, (q, q), 1)
    causal = ri >= cj
    tri = causal.astype(f32)
    acs = jnp.dot(tri, dt * a, preferred_element_type=f32, precision=lax.Precision.HIGHEST)
    acs_t = acs.T
    eacs = jnp.exp(acs)
    dte = jnp.exp(acs[q - 1:q, :] - acs)

    e = e_ref[...]
    dt_x = _expand_heads(dt, e)
    eacs_x = _expand_heads(eacs, e)
    dte_x = _expand_heads(dte, e)

    xdt = xa * dt_x
    xd_end = (xdt * dte_x).astype(MM)
    xdt_m = xdt.astype(MM)
    lane = lax.broadcasted_iota(jnp.int32, (q, LANES), 1)
    lo_half = lane < SSM_HEAD_DIM

    heads_per_group = SSM_HEADS // SSM_GROUPS
    for g in range(SSM_GROUPS):
        gs = slice(g * GROUP_WIDTH, (g + 1) * GROUP_WIDTH)
        bg = ba[:, g * SSM_STATE:(g + 1) * SSM_STATE]
        cg = ca[:, g * SSM_STATE:(g + 1) * SSM_STATE]
        cb = _dot_nt(cg, bg)
        h_in = state[g]
        y_off = _dot(cg, h_in) * eacs_x[:, gs]
        s_new = _dot_tn(bg, xd_end[:, gs])
        state[g] = h_in * eacs_x[q - 1:q, gs] + s_new
        for pr in range(heads_per_group // 2):
            h0 = g * heads_per_group + 2 * pr
            ms = []
            for hh in (h0, h0 + 1):
                diff = acs[:, hh:hh + 1] - acs_t[hh:hh + 1, :]
                lmat = jnp.exp(jnp.where(causal, diff, -jnp.inf))
                ms.append((cb * lmat).astype(MM))
            lhs = jnp.concatenate(ms, axis=1)
            xp = xdt_m[:, h0 * SSM_HEAD_DIM:(h0 + 2) * SSM_HEAD_DIM]
            zero = jnp.zeros_like(xp)
            rhs = jnp.concatenate([jnp.where(lo_half, xp, zero), jnp.where(lo_half, zero, xp)], axis=0)
            yd = jnp.dot(lhs, rhs, preferred_element_type=f32)
            cs = slice(h0 * SSM_HEAD_DIM, (h0 + 2) * SSM_HEAD_DIM)
            ybuf[:, cs] = yd + y_off[:, 2 * pr * SSM_HEAD_DIM:(2 * pr + 2) * SSM_HEAD_DIM]

    y = ybuf[...] + dsk_ref[...] * xa
    gz = y * _silu(z_ref[...])
    for g in range(SSM_GROUPS):
        gs = slice(g * GROUP_WIDTH, (g + 1) * GROUP_WIDTH)
        gg = gz[:, gs]
        ms = jnp.mean(gg * gg, axis=-1, keepdims=True)
        o_ref[:, gs] = (gg * lax.rsqrt(ms + EPS) * nw_ref[:, gs]).astype(o_ref.dtype)


def _ssd(proj, lead, bsz, tp, cwx, cwb, cwc, cbx, cbb, cbc, dtb, alog, dsk, nw):
    nc = tp // ROW_BLOCK
    q = ROW_BLOCK

    def seg(name):
        w = _DST[name][1]
        cbk = _col_block(name)
        return pl.BlockSpec((q, w), lambda b, c: (b * nc + c, cbk))

    def full(a):
        return pl.BlockSpec(a.shape, lambda b, c: (0, 0))

    small = (cwx, cwb, cwc, cbx, cbb, cbc, dtb, alog, dsk, nw)
    return pl.pallas_call(
        functools.partial(_ssd_kernel, lead),
        grid=(bsz, nc),
        in_specs=[seg("z"), seg("x"), seg("b"), seg("c"), seg("dt")] + [full(a) for a in small],
        out_specs=pl.BlockSpec((q, D_INNER), lambda b, c: (b * nc + c, 0)),
        out_shape=jax.ShapeDtypeStruct((bsz * tp, D_INNER), MM),
        scratch_shapes=[pltpu.VMEM((q + 8, D_INNER), jnp.float32),
                        pltpu.VMEM((q + 8, BC_WIDTH), jnp.float32),
                        pltpu.VMEM((q + 8, BC_WIDTH), jnp.float32),
                        pltpu.VMEM((SSM_GROUPS, SSM_STATE, GROUP_WIDTH), jnp.float32),
                        pltpu.VMEM((q, D_INNER), jnp.float32),
                        pltpu.VMEM((LANES, D_INNER), jnp.bfloat16)],
        compiler_params=_params(("parallel", "arbitrary")),
        name="ssd",
    )(proj, proj, proj, proj, proj, *small)


def _split2(a):
    hi = a.astype(jnp.bfloat16)
    lo = (a - hi.astype(jnp.float32)).astype(jnp.bfloat16)
    return hi, lo


def _dot2(a, b):
    hi, lo = _split2(a)
    f = functools.partial(jnp.dot, preferred_element_type=jnp.float32)
    return f(hi, b) + f(lo, b)


def _rot_half_matrix(head_dim):
    half = head_dim // ROPE_FRACTION // 2
    rj = lax.broadcasted_iota(jnp.int32, (LANES, LANES), 0)
    ci = lax.broadcasted_iota(jnp.int32, (LANES, LANES), 1)
    cm = ci % head_dim
    neg = (cm < half) & (rj == ci + half)
    pos = (cm >= half) & (cm < 2 * half) & (rj == ci - half)
    return jnp.where(neg, -1.0, jnp.where(pos, 1.0, 0.0)).astype(jnp.bfloat16)


def _rope128(x, tab, half):
    cos = tab[:, 0:LANES]
    s_a = tab[:, LANES:2 * LANES]
    s_b = tab[:, 2 * LANES:3 * LANES]
    return x * cos + pltpu.roll(x, LANES - half, 1) * s_a + pltpu.roll(x, half, 1) * s_b


def _rms128(x, w, width):
    ms = jnp.sum(x * x, axis=-1, keepdims=True) * (1.0 / width)
    return x * lax.rsqrt(ms + EPS) * w


def _kprep_kernel(k_ref, v_ref, ik_ref, ta_ref, ti_ref, knw_ref, iknw_ref,
                  ko_ref, vo_ref, iklo_ref, ikhi_ref):
    ta = ta_ref[...]
    for h in range(ATTN_KV_HEADS):
        cs = slice(h * ATTN_HEAD_DIM, (h + 1) * ATTN_HEAD_DIM)
        kn = _rms128(k_ref[:, cs], knw_ref[...], ATTN_HEAD_DIM)
        ko_ref[:, cs] = _rope128(kn, ta, ATTN_HEAD_DIM // ROPE_FRACTION // 2).astype(ko_ref.dtype)
    vo_ref[...] = v_ref[...].astype(vo_ref.dtype)
    ikn = _rms128(ik_ref[...], iknw_ref[...], IDX_DIM)
    ikn = _rope128(ikn, ti_ref[...], IDX_DIM // ROPE_FRACTION // 2)
    iklo_ref[...] = ikn.astype(iklo_ref.dtype)
    ikhi_ref[...] = pltpu.roll(ikn, IDX_DIM, 1).astype(ikhi_ref.dtype)


def _kprep(proj, tab_a, tab_i, knw, iknw, tp):
    m = proj.shape[0]
    tr = _pick_tile(tp, 1408, 8)
    nt = tp // tr

    def seg(name):
        w = _DST[name][1]
        cbk = _col_block(name)
        return pl.BlockSpec((tr, w), lambda i: (i, cbk))

    tab = pl.BlockSpec((tr, 3 * LANES), lambda i: (i % nt, 0))
    vec = pl.BlockSpec((1, LANES), lambda i: (0, 0))
    row = lambda w: pl.BlockSpec((tr, w), lambda i: (i, 0))
    return pl.pallas_call(
        _kprep_kernel,
        grid=(m // tr,),
        in_specs=[seg("k"), seg("v"), seg("ik"), tab, tab, vec, vec],
        out_specs=[row(KV_WIDTH), row(KV_WIDTH), row(LANES), row(LANES)],
        out_shape=[jax.ShapeDtypeStruct((m, KV_WIDTH), MM), jax.ShapeDtypeStruct((m, KV_WIDTH), MM),
                   jax.ShapeDtypeStruct((m, LANES), MM), jax.ShapeDtypeStruct((m, LANES), MM)],
        compiler_params=_params(("parallel",)),
        name="kprep",
    )(proj, proj, proj, tab_a, tab_i, knw, iknw)


def _attn_kernel(lead, k_sel, tk, tkw, q_ref, iq_ref, iw_ref, az_ref, ta_ref, ti_ref, qnw_ref,
                 kn_ref, vb_ref, iklo_ref, ikhi_ref, o_ref, key_ref, iwb_ref, s_ref):
    f32 = jnp.float32
    tq = ROW_BLOCK
    sb = tk // tq
    ts = sb * tq
    nlb = tk // LANES
    qi = pl.program_id(1)
    si = qi // sb

    @pl.when(qi % sb == 0)
    def _():
        nch = si + 1
        ti = ti_ref[...]
        npair = IDX_HEADS // 2
        xi = jnp.concatenate([iq_ref[:, p * LANES:(p + 1) * LANES] for p in range(npair)], axis=0)
        roti = _dot2(xi, _rot_half_matrix(IDX_DIM))
        cos_i, sin_i = ti[:, 0:LANES], ti[:, 2 * LANES:3 * LANES] - ti[:, LANES:2 * LANES]
        iq_all = xi.reshape(npair, ts, LANES) * cos_i[None] + roti.reshape(npair, ts, LANES) * sin_i[None]
        iq_all = iq_all.astype(MM).reshape(npair * ts, LANES)
        iw = iw_ref[...] * (IDX_HEADS ** -0.5 * IDX_DIM ** -0.5)
        for h in range(IDX_HEADS):
            iwb_ref[h] = jnp.broadcast_to(iw[:, h:h + 1], (ts, LANES))

        qrow = si * ts + lax.broadcasted_iota(jnp.int32, (ts, 1), 0)
        kcol0 = lax.broadcasted_iota(jnp.int32, (1, tk), 1)

        def score_chunk(j, carry):
            off = pl.multiple_of(j * tk, tk)
            l_even = _dot_nt(iq_all, iklo_ref[pl.ds(off, tk), :])
            l_odd = _dot_nt(iq_all, ikhi_ref[pl.ds(off, tk), :])
            cols = []
            for c in range(nlb):
                ls = slice(c * LANES, (c + 1) * LANES)
                sc = None
                for p in range(npair):
                    rs = slice(p * ts, (p + 1) * ts)
                    term = (iwb_ref[2 * p] * jnp.maximum(l_even[rs, ls], 0.0)
                            + iwb_ref[2 * p + 1] * jnp.maximum(l_odd[rs, ls], 0.0))
                    sc = term if sc is None else sc + term
                cols.append(sc)
            sc = jnp.concatenate(cols, axis=1)
            bits = pltpu.bitcast(sc, jnp.int32)
            keys = bits ^ ((bits >> 31) & jnp.int32(0x7FFFFFFF))
            kcol = off + kcol0
            visible = (kcol <= qrow) & (kcol >= lead)
            key_ref[:, pl.ds(off, tk)] = jnp.where(visible, keys, jnp.int32(INT_MIN))
            return carry

        lax.fori_loop(0, nch, score_chunk, 0)

        ones_cnt = jnp.ones((LANES, LANES), jnp.bfloat16)

        def bit_step(bi, t):
            bit = lax.shift_left(jnp.int32(1), 31 - bi)
            cand = t | bit
            cand_s = cand ^ jnp.int32(INT_MIN)
            accs = []
            for r in range(sb):
                rs = slice(r * tq, (r + 1) * tq)
                cand_r = cand_s[rs]

                def count_chunk(j, acc, rs=rs, cand_r=cand_r):
                    off = pl.multiple_of(j * tk, tk)
                    for c in range(nlb):
                        kk = key_ref[rs, pl.ds(pl.multiple_of(off + c * LANES, LANES), LANES)]
                        acc = acc + jnp.where(kk >= cand_r, 1.0, 0.0)
                    return acc

                accs.append(lax.fori_loop(0, nch, count_chunk, jnp.zeros((tq, LANES), f32)))
            acc = jnp.concatenate(accs, axis=0).astype(jnp.bfloat16)
            cnt = jnp.dot(acc, ones_cnt, preferred_element_type=f32)
            return jnp.where(cnt >= float(k_sel), cand, t)

        t_u = lax.fori_loop(0, 32, bit_step, jnp.zeros((ts, LANES), jnp.int32))
        thr = jnp.maximum(t_u ^ jnp.int32(INT_MIN), jnp.int32(INT_MIN + 1))

        def bias_chunk(j, carry):
            off = pl.multiple_of(j * tk, tk)
            for c in range(nlb):
                cols = pl.ds(pl.multiple_of(off + c * LANES, LANES), LANES)
                kk = key_ref[:, cols]
                key_ref[:, cols] = pltpu.bitcast(jnp.where(kk >= thr, 0.0, NEG_BIG), jnp.int32)
            return carry

        lax.fori_loop(0, nch, bias_chunk, 0)

    ta = ta_ref[...]
    cos_a, sin_a = ta[:, 0:LANES], ta[:, 2 * LANES:3 * LANES] - ta[:, LANES:2 * LANES]
    ones_m = jnp.ones((LANES, LANES), jnp.bfloat16)
    rot_m = _rot_half_matrix(ATTN_HEAD_DIM)
    scale = ATTN_HEAD_DIM ** -0.5 * LOG2E

    def prep_q(g):
        heads = range(g * KV_GROUP, (g + 1) * KV_GROUP)
        xs = jnp.concatenate([q_ref[:, h * ATTN_HEAD_DIM:(h + 1) * ATTN_HEAD_DIM] for h in heads], axis=0)
        ms = _dot2(xs * xs, ones_m) * (1.0 / ATTN_HEAD_DIM)
        xn = xs * lax.rsqrt(ms + EPS) * qnw_ref[...]
        rot = _dot2(xn, rot_m)
        qr = xn.reshape(KV_GROUP, tq, LANES) * cos_a[None] + rot.reshape(KV_GROUP, tq, LANES) * sin_a[None]
        return (qr * scale).astype(MM).reshape(KV_GROUP * tq, LANES)

    rows4 = KV_GROUP * tq
    brow = pl.multiple_of((qi % sb) * tq, tq)
    macc0 = jnp.full((rows4, LANES), NEG_BIG, f32)
    acc0 = jnp.zeros((rows4, 2 * ATTN_HEAD_DIM), f32)

    def logits_chunk(g, q4, off, w, macc):
        cs = slice(g * ATTN_HEAD_DIM, (g + 1) * ATTN_HEAD_DIM)
        s = _dot_nt(q4, kn_ref[pl.ds(off, w), cs])
        bias = pltpu.bitcast(key_ref[pl.ds(brow, tq), pl.ds(off, w)], f32)
        s = (s.reshape(KV_GROUP, tq, w) + bias[None]).reshape(rows4, w)
        s_ref[g % 2, :, pl.ds(off, w)] = s
        for c in range(w // LANES):
            macc = jnp.maximum(macc, s[:, c * LANES:(c + 1) * LANES])
        return macc

    def pv_chunk(g, m, off, w, acc):
        cs = slice(g * ATTN_HEAD_DIM, (g + 1) * ATTN_HEAD_DIM)
        p = jnp.exp2(s_ref[g % 2, :, pl.ds(off, w)] - m).astype(MM)
        v1 = jnp.concatenate([vb_ref[pl.ds(off, w), cs], jnp.ones((w, ATTN_HEAD_DIM), MM)], axis=1)
        return acc + jnp.dot(p, v1, preferred_element_type=f32)

    needed = qi * tq + tq
    nwide = needed // tkw
    nnarrow = (needed - nwide * tkw + tk - 1) // tk

    def sweep(fn, carry):
        carry = lax.fori_loop(0, nwide, lambda j, c: fn(pl.multiple_of(j * tkw, tkw), tkw, c), carry)
        return lax.fori_loop(
            0, nnarrow, lambda j, c: fn(pl.multiple_of(nwide * tkw + j * tk, tk), tk, c), carry)

    macc = sweep(functools.partial(logits_chunk, 0, prep_q(0)), macc0)
    for g in range(ATTN_KV_HEADS):
        m = jnp.max(macc, axis=-1, keepdims=True)
        if g + 1 < ATTN_KV_HEADS:
            q4n = prep_q(g + 1)

            def both(off, w, carry, g=g, m=m, q4n=q4n):
                return pv_chunk(g, m, off, w, carry[0]), logits_chunk(g + 1, q4n, off, w, carry[1])
            acc, macc = sweep(both, (acc0, macc0))
        else:
            acc = sweep(functools.partial(pv_chunk, g, m), acc0)
        o = acc[:, :ATTN_HEAD_DIM] / acc[:, ATTN_HEAD_DIM:]
        for r in range(KV_GROUP):
            h = g * KV_GROUP + r
            hs = slice(h * ATTN_HEAD_DIM, (h + 1) * ATTN_HEAD_DIM)
            o_ref[:, hs] = (o[r * tq:(r + 1) * tq, :] * _silu(az_ref[:, hs])).astype(o_ref.dtype)


def _attn(proj, kn, vb, iklo, ikhi, tab_a, tab_i, qnw, lead, k_sel, bsz, tp):
    tq = ROW_BLOCK
    nq = tp // tq
    sb = _pick_tile(nq, 3, 1)
    tk = sb * tq
    tkw = ATTN_WIDE_CHUNKS * tk
    ns = nq // sb

    def seg(name):
        w = _DST[name][1]
        cbk = _col_block(name)
        return pl.BlockSpec((tq, w), lambda b, i: (b * nq + i, cbk))

    def seg_sb(name):
        w = _DST[name][1]
        cbk = _col_block(name)
        return pl.BlockSpec((tk, w), lambda b, i: (b * ns + i // sb, cbk))

    tab = pl.BlockSpec((tq, 3 * LANES), lambda b, i: (i, 0))
    tab_sb = pl.BlockSpec((tk, 3 * LANES), lambda b, i: (i // sb, 0))
    vec = pl.BlockSpec((1, LANES), lambda b, i: (0, 0))
    keys = lambda w: pl.BlockSpec((tp, w), lambda b, i: (b, 0), pipeline_mode=pl.Buffered(1))
    return pl.pallas_call(
        functools.partial(_attn_kernel, lead, k_sel, tk, tkw),
        grid=(bsz, nq),
        in_specs=[seg("q"), seg_sb("iq"), seg_sb("iw"), seg("az"), tab, tab_sb, vec,
                  keys(KV_WIDTH), keys(KV_WIDTH), keys(LANES), keys(LANES)],
        out_specs=pl.BlockSpec((tq, ATTN_WIDTH), lambda b, i: (b * nq + i, 0)),
        out_shape=jax.ShapeDtypeStruct((bsz * tp, ATTN_WIDTH), MM),
        scratch_shapes=[pltpu.VMEM((tk, tp), jnp.int32),
                        pltpu.VMEM((IDX_HEADS, tk, LANES), jnp.float32),
                        pltpu.VMEM((2, KV_GROUP * tq, tp), jnp.float32)],
        compiler_params=_params(("parallel", "arbitrary")),
        name="attn",
    )(proj, proj, proj, proj, tab_a, tab_i, qnw, kn, vb, iklo, ikhi)


def _merge_kernel(l_ref, ys_ref, ya_ref, gs_ref, ga_ref, ws_ref, wa_ref, o_ref):
    del l_ref
    ys = jnp.dot(ys_ref[...], ws_ref[...], preferred_element_type=jnp.float32)
    ya = jnp.dot(ya_ref[...], wa_ref[...], preferred_element_type=jnp.float32)
    o_ref[...] = (_sigmoid(gs_ref[...]) * ys + _sigmoid(ga_ref[...]) * ya).astype(o_ref.dtype)


def _merge(l_arr, ys, ya, proj, ws, wa):
    m = ys.shape[0]
    tm = _pick_tile(m, 1056, 8)
    tn = 512
    gsb = _DST["gs"][0] // tn
    gab = _DST["ga"][0] // tn
    return pl.pallas_call(
        _merge_kernel,
        grid_spec=pltpu.PrefetchScalarGridSpec(
            num_scalar_prefetch=1, grid=(m // tm, D_MODEL // tn),
            in_specs=[pl.BlockSpec((tm, D_INNER), lambda i, j, l: (i, 0)),
                      pl.BlockSpec((tm, ATTN_WIDTH), lambda i, j, l: (i, 0)),
                      pl.BlockSpec((tm, tn), lambda i, j, l: (i, gsb + j)),
                      pl.BlockSpec((tm, tn), lambda i, j, l: (i, gab + j)),
                      pl.BlockSpec((None, D_INNER, tn), lambda i, j, l: (l[0], 0, j)),
                      pl.BlockSpec((None, ATTN_WIDTH, tn), lambda i, j, l: (l[0], 0, j))],
            out_specs=pl.BlockSpec((tm, tn), lambda i, j, l: (i, j))),
        out_shape=jax.ShapeDtypeStruct((m, D_MODEL), MM),
        compiler_params=_params(("parallel", "arbitrary")),
        name="merge",
    )(l_arr, ys, ya, proj, proj, ws, wa)


def _outproj_kernel(lead, tp, l_ref, mg_ref, h_ref, w_ref, o_ref):
    del l_ref
    tm = mg_ref.shape[0]
    upd = jnp.dot(mg_ref[...], w_ref[...], preferred_element_type=jnp.float32)
    rows = pl.program_id(0) * tm + lax.broadcasted_iota(jnp.int32, (tm, 1), 0)
    o_ref[...] = jnp.where(rows % tp >= lead, h_ref[...] + upd, 0.0)


def _outproj(l_arr, mg, h2, wo, lead, tp):
    m = mg.shape[0]
    tm = _pick_tile(m, 1056, 8)
    tn = 512
    return pl.pallas_call(
        functools.partial(_outproj_kernel, lead, tp),
        grid_spec=pltpu.PrefetchScalarGridSpec(
            num_scalar_prefetch=1, grid=(m // tm, D_MODEL // tn),
            in_specs=[pl.BlockSpec((tm, D_MODEL), lambda i, j, l: (i, 0)),
                      pl.BlockSpec((tm, tn), lambda i, j, l: (i, j)),
                      pl.BlockSpec((None, D_MODEL, tn), lambda i, j, l: (l[0], 0, j))],
            out_specs=pl.BlockSpec((tm, tn), lambda i, j, l: (i, j))),
        out_shape=jax.ShapeDtypeStruct((m, D_MODEL), jnp.float32),
        input_output_aliases={2: 0},
        compiler_params=_params(("parallel", "arbitrary")),
        name="outproj",
    )(l_arr, mg, h2, wo)


def _rope_table(pos, head_dim):
    rot = head_dim // ROPE_FRACTION
    half = rot // 2
    inv = ROPE_THETA ** (-jnp.arange(0, rot, 2, dtype=jnp.float32) / rot)
    ang = pos.astype(jnp.float32)[:, None] * inv[None, :]
    cos, sin = jnp.cos(ang), jnp.sin(ang)
    n = pos.shape[0]
    rest = head_dim - rot
    c = jnp.concatenate([cos, cos, jnp.ones((n, rest), jnp.float32)], axis=1)
    s_a = jnp.concatenate([-sin, jnp.zeros((n, head_dim - half), jnp.float32)], axis=1)
    s_b = jnp.concatenate([jnp.zeros((n, half), jnp.float32), sin, jnp.zeros((n, rest), jnp.float32)], axis=1)
    rep = LANES // head_dim
    return jnp.concatenate([jnp.tile(c, (1, rep)), jnp.tile(s_a, (1, rep)), jnp.tile(s_b, (1, rep))], axis=1)


def _pad_lanes(a, width=LANES):
    return jnp.pad(a, [(0, 0)] * (a.ndim - 1) + [(0, width - a.shape[-1])])


def _prep_w_in(w_in):
    depth, d, _ = w_in.shape
    parts = []
    for name, (_, w) in _DST.items():
        s0, sw = _SRC[name]
        parts.append(_pad_lanes(w_in[:, :, s0:s0 + sw], w))
    used = sum(w for _, w in _DST.values())
    parts.append(jnp.zeros((depth, d, NP - used), w_in.dtype))
    return jnp.concatenate(parts, axis=-1).astype(MM)


def kernel(x, meta_tokens, norm_w, w_in, conv_w, conv_b, dt_bias, a_log, d_skip, ssm_norm_w,
           w_ssm_out, q_norm_w, k_norm_w, idx_k_norm_w, w_attn_out, w_out):
    bsz, seq, d = x.shape
    depth = norm_w.shape[0]
    assert d == D_MODEL and w_in.shape[-1] == N_IN
    t = seq + N_META
    lead = (-t) % ROW_BLOCK
    tp = t + lead
    k_sel = min(TOPK_MAX, seq // 4)

    meta = jnp.broadcast_to(meta_tokens[None].astype(x.dtype), (bsz, N_META, d))
    h = jnp.concatenate([jnp.zeros((bsz, lead, d), x.dtype), meta, x], axis=1).reshape(bsz * tp, d)

    pos = jnp.arange(tp) - lead
    tab_a = _rope_table(pos, ATTN_HEAD_DIM)
    tab_i = _rope_table(pos, IDX_DIM)

    wp = _prep_w_in(w_in)
    ws = w_ssm_out.astype(MM)
    wa = w_attn_out.astype(MM)
    wo = w_out.astype(MM)
    cw_x, cw_b, cw_c = (conv_w[:, :, :D_INNER], conv_w[:, :, D_INNER:D_INNER + BC_WIDTH],
                        conv_w[:, :, D_INNER + BC_WIDTH:])
    cb_x, cb_b, cb_c = (conv_b[:, None, :D_INNER], conv_b[:, None, D_INNER:D_INNER + BC_WIDTH],
                        conv_b[:, None, D_INNER + BC_WIDTH:])
    dtb = _pad_lanes(dt_bias)[:, None, :]
    alog = _pad_lanes(a_log)[:, None, :]
    dsk = jnp.repeat(d_skip, SSM_HEAD_DIM, axis=-1)[:, None, :]
    snw = ssm_norm_w[:, None, :]
    iknw = _pad_lanes(idx_k_norm_w)[:, None, :]
    qnw = q_norm_w[:, None, :]
    knw = k_norm_w[:, None, :]
    nw = norm_w[:, None, :]

    def layer(l, h):
        l_arr = jnp.reshape(l, (1,)).astype(jnp.int32)
        proj = _inproj(l_arr, h, nw[l], wp)
        ys = _ssd(proj, lead, bsz, tp, cw_x[l], cw_b[l], cw_c[l], cb_x[l], cb_b[l], cb_c[l],
                  dtb[l], alog[l], dsk[l], snw[l])
        kn, vb, iklo, ikhi = _kprep(proj, tab_a, tab_i, knw[l], iknw[l], tp)
        ya = _attn(proj, kn, vb, iklo, ikhi, tab_a, tab_i, qnw[l], lead, k_sel, bsz, tp)
        mg = _merge(l_arr, ys, ya, proj, ws, wa)
        return _outproj(l_arr, mg, h, wo, lead, tp)

    h = lax.fori_loop(0, depth, layer, h)
    return h.reshape(bsz, tp, d)[:, lead + N_META:]
```

```python
import functools

import jax
import jax.numpy as jnp
from jax import lax
from jax.experimental import pallas as pl
from jax.experimental.pallas import tpu as pltpu

D_MODEL = 2048
N_META = 16
D_INNER = 2 * D_MODEL
SSM_HEAD_DIM = 64
SSM_HEADS = D_INNER // SSM_HEAD_DIM
SSM_GROUPS = 8
SSM_STATE = 128
CONV_WIDTH = 4
BC_WIDTH = SSM_GROUPS * SSM_STATE
GROUP_WIDTH = D_INNER // SSM_GROUPS
ATTN_HEADS = 16
ATTN_KV_HEADS = 4
ATTN_HEAD_DIM = 128
ATTN_WIDTH = ATTN_HEADS * ATTN_HEAD_DIM
KV_WIDTH = ATTN_KV_HEADS * ATTN_HEAD_DIM
KV_GROUP = ATTN_HEADS // ATTN_KV_HEADS
IDX_HEADS = 16
IDX_DIM = 64
IDX_WIDTH = IDX_HEADS * IDX_DIM
TOPK_MAX = 256
ROPE_THETA = 500000.0
ROPE_FRACTION = 4
EPS = 1e-6

LANES = 128
ROW_BLOCK = 128
ATTN_WIDE_CHUNKS = 4
VMEM_LIMIT = 56 * 1024 * 1024

MM = jnp.bfloat16
NEG_BIG = -1e30
LOG2E = 1.4426950408889634
INT_MIN = -2 ** 31

_SRC = {}
_o = 0
for _n, _w in (("z", D_INNER), ("x", D_INNER), ("b", BC_WIDTH), ("c", BC_WIDTH), ("dt", SSM_HEADS),
               ("q", ATTN_WIDTH), ("k", KV_WIDTH), ("v", KV_WIDTH), ("az", ATTN_WIDTH),
               ("iq", IDX_WIDTH), ("ik", IDX_DIM), ("iw", IDX_HEADS), ("gs", D_MODEL), ("ga", D_MODEL)):
    _SRC[_n] = (_o, _w)
    _o += _w
N_IN = _o

_DST = {}
_o = 0
for _n, _w in (("z", 4096), ("x", 4096), ("q", 2048), ("az", 2048), ("gs", 2048), ("ga", 2048),
               ("b", 1024), ("c", 1024), ("iq", 1024), ("k", 512), ("v", 512),
               ("dt", 128), ("ik", 128), ("iw", 128)):
    assert _o % _w == 0
    _DST[_n] = (_o, _w)
    _o += _w
PROJ_TILE_N = 512
NP = -(-_o // PROJ_TILE_N) * PROJ_TILE_N


def _col_block(name):
    off, w = _DST[name]
    return off // w


def _pick_tile(n, target, mult):
    best = None
    for t in range(mult, min(n, target) + 1, mult):
        if n % t == 0:
            best = t
    assert best is not None, (n, target, mult)
    return best


def _sigmoid(v):
    return 0.5 + 0.5 * jnp.tanh(0.5 * v)


def _silu(v):
    h = 0.5 * v
    return h + h * jnp.tanh(h)


def _dot(a, b):
    return jnp.dot(a.astype(MM), b.astype(MM), preferred_element_type=jnp.float32)


def _dot_nt(a, b):
    return lax.dot_general(a.astype(MM), b.astype(MM), (((1,), (1,)), ((), ())),
                           preferred_element_type=jnp.float32)


def _dot_tn(a, b):
    return lax.dot_general(a.astype(MM), b.astype(MM), (((0,), (0,)), ((), ())),
                           preferred_element_type=jnp.float32)


def _params(sem):
    return pltpu.CompilerParams(dimension_semantics=sem, vmem_limit_bytes=VMEM_LIMIT)


def _inproj_kernel(l_ref, h_ref, nw_ref, w_ref, o_ref, hn_ref):
    del l_ref

    @pl.when(pl.program_id(1) == 0)
    def _():
        x = h_ref[...]
        ms = jnp.mean(x * x, axis=-1, keepdims=True)
        hn_ref[...] = (x * lax.rsqrt(ms + EPS) * nw_ref[...]).astype(hn_ref.dtype)

    o_ref[...] = jnp.dot(hn_ref[...], w_ref[...], preferred_element_type=jnp.float32)


def _inproj(l_arr, h2, nw, wp):
    m, d = h2.shape
    tm = _pick_tile(m, 1056, 8)
    tn = PROJ_TILE_N
    return pl.pallas_call(
        _inproj_kernel,
        grid_spec=pltpu.PrefetchScalarGridSpec(
            num_scalar_prefetch=1, grid=(m // tm, NP // tn),
            in_specs=[pl.BlockSpec((tm, d), lambda i, j, l: (i, 0)),
                      pl.BlockSpec((1, d), lambda i, j, l: (0, 0)),
                      pl.BlockSpec((None, d, tn), lambda i, j, l: (l[0], 0, j))],
            out_specs=pl.BlockSpec((tm, tn), lambda i, j, l: (i, j)),
            scratch_shapes=[pltpu.VMEM((tm, d), MM)]),
        out_shape=jax.ShapeDtypeStruct((m, NP), jnp.float32),
        compiler_params=_params(("parallel", "arbitrary")),
        name="inproj",
    )(l_arr, h2, nw, wp)


def _split3(a):
    hi = a.astype(jnp.bfloat16)
    r1 = a - hi.astype(jnp.float32)
    mid = r1.astype(jnp.bfloat16)
    lo = (r1 - mid.astype(jnp.float32)).astype(jnp.bfloat16)
    return hi, mid, lo


def _expand_heads(a, e):
    hi, mid, lo = _split3(a)
    f = functools.partial(jnp.dot, preferred_element_type=jnp.float32)
    return f(hi, e) + f(mid, e) + f(lo, e)


def _ssd_kernel(lead, z_ref, x_ref, b_ref, c_ref, dt_ref, cwx_ref, cwb_ref, cwc_ref,
                cbx_ref, cbb_ref, cbc_ref, dtb_ref, alog_ref, dsk_ref, nw_ref,
                o_ref, extx, extb, extc, state, ybuf, e_ref):
    f32 = jnp.float32
    ci = pl.program_id(1)
    q = ROW_BLOCK

    @pl.when(ci == 0)
    def _():
        extx[0:8, :] = jnp.zeros((8, D_INNER), f32)
        extb[0:8, :] = jnp.zeros((8, BC_WIDTH), f32)
        extc[0:8, :] = jnp.zeros((8, BC_WIDTH), f32)
        state[...] = jnp.zeros(state.shape, f32)
        hrow = lax.broadcasted_iota(jnp.int32, (LANES, D_INNER), 0)
        hcol = lax.broadcasted_iota(jnp.int32, (LANES, D_INNER), 1) // SSM_HEAD_DIM
        e_ref[...] = (hrow == hcol).astype(jnp.bfloat16)

    def conv_silu(u_ref, ext, w_ref, bias_ref):
        ext[8:8 + q, :] = u_ref[...]
        acc = bias_ref[...] + w_ref[CONV_WIDTH - 1:CONV_WIDTH, :] * ext[8:8 + q, :]
        for k in range(CONV_WIDTH - 1):
            s = 8 - (CONV_WIDTH - 1) + k
            acc = acc + w_ref[k:k + 1, :] * ext[s:s + q, :]
        ext[0:8, :] = ext[q:q + 8, :]
        return _silu(acc)

    rows = ci * q + lax.broadcasted_iota(jnp.int32, (q, 1), 0)
    valid = rows >= lead

    xa = jnp.where(valid, conv_silu(x_ref, extx, cwx_ref, cbx_ref), 0.0)
    ba = conv_silu(b_ref, extb, cwb_ref, cbb_ref)
    ca = conv_silu(c_ref, extc, cwc_ref, cbc_ref)

    dtr = dt_ref[...] + dtb_ref[...]
    dt = jnp.maximum(dtr, 0.0) + jnp.log(1.0 + jnp.exp(-jnp.abs(dtr)))
    dt = jnp.where(valid, dt, 0.0)
    a = -jnp.exp(alog_ref[...])
    ri = lax.broadcasted_iota(jnp.int32, (q, q), 0)
    cj = lax.broadcasted_iota(jnp.int32, (q, q), 1)
    causal = ri >= cj
    tri = causal.astype(f32)
    acs = jnp.dot(tri, dt * a, preferred_element_type=f32, precision=lax.Precision.HIGHEST)
    acs_t = acs.T
    eacs = jnp.exp(acs)
    dte = jnp.exp(acs[q - 1:q, :] - acs)

    e = e_ref[...]
    dt_x = _expand_heads(dt, e)
    eacs_x = _expand_heads(eacs, e)
    dte_x = _expand_heads(dte, e)

    xdt = xa * dt_x
    xd_end = (xdt * dte_x).astype(MM)
    xdt_m = xdt.astype(MM)
    lane = lax.broadcasted_iota(jnp.int32, (q, LANES), 1)
    lo_half = lane < SSM_HEAD_DIM

    heads_per_group = SSM_HEADS // SSM_GROUPS
    for g in range(SSM_GROUPS):
        gs = slice(g * GROUP_WIDTH, (g + 1) * GROUP_WIDTH)
        bg = ba[:, g * SSM_STATE:(g + 1) * SSM_STATE]
        cg = ca[:, g * SSM_STATE:(g + 1) * SSM_STATE]
        cb = _dot_nt(cg, bg)
        h_in = state[g]
        y_off = _dot(cg, h_in) * eacs_x[:, gs]
        s_new = _dot_tn(bg, xd_end[:, gs])
        state[g] = h_in * eacs_x[q - 1:q, gs] + s_new
        for pr in range(heads_per_group // 2):
            h0 = g * heads_per_group + 2 * pr
            ms = []
            for hh in (h0, h0 + 1):
                diff = acs[:, hh:hh + 1] - acs_t[hh:hh + 1, :]
                lmat = jnp.exp(jnp.where(causal, diff, -jnp.inf))
                ms.append((cb * lmat).astype(MM))
            lhs = jnp.concatenate(ms, axis=1)
            xp = xdt_m[:, h0 * SSM_HEAD_DIM:(h0 + 2) * SSM_HEAD_DIM]
            zero = jnp.zeros_like(xp)
            rhs = jnp.concatenate([jnp.where(lo_half, xp, zero), jnp.where(lo_half, zero, xp)], axis=0)
            yd = jnp.dot(lhs, rhs, preferred_element_type=f32)
            cs = slice(h0 * SSM_HEAD_DIM, (h0 + 2) * SSM_HEAD_DIM)
            ybuf[:, cs] = yd + y_off[:, 2 * pr * SSM_HEAD_DIM:(2 * pr + 2) * SSM_HEAD_DIM]

    y = ybuf[...] + dsk_ref[...] * xa
    gz = y * _silu(z_ref[...])
    for g in range(SSM_GROUPS):
        gs = slice(g * GROUP_WIDTH, (g + 1) * GROUP_WIDTH)
        gg = gz[:, gs]
        ms = jnp.mean(gg * gg, axis=-1, keepdims=True)
        o_ref[:, gs] = (gg * lax.rsqrt(ms + EPS) * nw_ref[:, gs]).astype(o_ref.dtype)


def _ssd(proj, lead, bsz, tp, cwx, cwb, cwc, cbx, cbb, cbc, dtb, alog, dsk, nw):
    nc = tp // ROW_BLOCK
    q = ROW_BLOCK

    def seg(name):
        w = _DST[name][1]
        cbk = _col_block(name)
        return pl.BlockSpec((q, w), lambda b, c: (b * nc + c, cbk))

    def full(a):
        return pl.BlockSpec(a.shape, lambda b, c: (0, 0))

    small = (cwx, cwb, cwc, cbx, cbb, cbc, dtb, alog, dsk, nw)
    return pl.pallas_call(
        functools.partial(_ssd_kernel, lead),
        grid=(bsz, nc),
        in_specs=[seg("z"), seg("x"), seg("b"), seg("c"), seg("dt")] + [full(a) for a in small],
        out_specs=pl.BlockSpec((q, D_INNER), lambda b, c: (b * nc + c, 0)),
        out_shape=jax.ShapeDtypeStruct((bsz * tp, D_INNER), MM),
        scratch_shapes=[pltpu.VMEM((q + 8, D_INNER), jnp.float32),
                        pltpu.VMEM((q + 8, BC_WIDTH), jnp.float32),
                        pltpu.VMEM((q + 8, BC_WIDTH), jnp.float32),
                        pltpu.VMEM((SSM_GROUPS, SSM_STATE, GROUP_WIDTH), jnp.float32),
                        pltpu.VMEM((q, D_INNER), jnp.float32),
                        pltpu.VMEM((LANES, D_INNER), jnp.bfloat16)],
        compiler_params=_params(("parallel", "arbitrary")),
        name="ssd",
    )(proj, proj, proj, proj, proj, *small)


def _split2(a):
    hi = a.astype(jnp.bfloat16)
    lo = (a - hi.astype(jnp.float32)).astype(jnp.bfloat16)
    return hi, lo


def _dot2(a, b):
    hi, lo = _split2(a)
    f = functools.partial(jnp.dot, preferred_element_type=jnp.float32)
    return f(hi, b) + f(lo, b)


def _rot_half_matrix(head_dim):
    half = head_dim // ROPE_FRACTION // 2
    rj = lax.broadcasted_iota(jnp.int32, (LANES, LANES), 0)
    ci = lax.broadcasted_iota(jnp.int32, (LANES, LANES), 1)
    cm = ci % head_dim
    neg = (cm < half) & (rj == ci + half)
    pos = (cm >= half) & (cm < 2 * half) & (rj == ci - half)
    return jnp.where(neg, -1.0, jnp.where(pos, 1.0, 0.0)).astype(jnp.bfloat16)


def _rope128(x, tab, half):
    cos = tab[:, 0:LANES]
    s_a = tab[:, LANES:2 * LANES]
    s_b = tab[:, 2 * LANES:3 * LANES]
    return x * cos + pltpu.roll(x, LANES - half, 1) * s_a + pltpu.roll(x, half, 1) * s_b


def _rms128(x, w, width):
    ms = jnp.sum(x * x, axis=-1, keepdims=True) * (1.0 / width)
    return x * lax.rsqrt(ms + EPS) * w


def _kprep_kernel(k_ref, v_ref, ik_ref, ta_ref, ti_ref, knw_ref, iknw_ref,
                  ko_ref, vo_ref, iklo_ref, ikhi_ref):
    ta = ta_ref[...]
    for h in range(ATTN_KV_HEADS):
        cs = slice(h * ATTN_HEAD_DIM, (h + 1) * ATTN_HEAD_DIM)
        kn = _rms128(k_ref[:, cs], knw_ref[...], ATTN_HEAD_DIM)
        ko_ref[:, cs] = _rope128(kn, ta, ATTN_HEAD_DIM // ROPE_FRACTION // 2).astype(ko_ref.dtype)
    vo_ref[...] = v_ref[...].astype(vo_ref.dtype)
    ikn = _rms128(ik_ref[...], iknw_ref[...], IDX_DIM)
    ikn = _rope128(ikn, ti_ref[...], IDX_DIM // ROPE_FRACTION // 2)
    iklo_ref[...] = ikn.astype(iklo_ref.dtype)
    ikhi_ref[...] = pltpu.roll(ikn, IDX_DIM, 1).astype(ikhi_ref.dtype)


def _kprep(proj, tab_a, tab_i, knw, iknw, tp):
    m = proj.shape[0]
    tr = _pick_tile(tp, 1408, 8)
    nt = tp // tr

    def seg(name):
        w = _DST[name][1]
        cbk = _col_block(name)
        return pl.BlockSpec((tr, w), lambda i: (i, cbk))

    tab = pl.BlockSpec((tr, 3 * LANES), lambda i: (i % nt, 0))
    vec = pl.BlockSpec((1, LANES), lambda i: (0, 0))
    row = lambda w: pl.BlockSpec((tr, w), lambda i: (i, 0))
    return pl.pallas_call(
        _kprep_kernel,
        grid=(m // tr,),
        in_specs=[seg("k"), seg("v"), seg("ik"), tab, tab, vec, vec],
        out_specs=[row(KV_WIDTH), row(KV_WIDTH), row(LANES), row(LANES)],
        out_shape=[jax.ShapeDtypeStruct((m, KV_WIDTH), MM), jax.ShapeDtypeStruct((m, KV_WIDTH), MM),
                   jax.ShapeDtypeStruct((m, LANES), MM), jax.ShapeDtypeStruct((m, LANES), MM)],
        compiler_params=_params(("parallel",)),
        name="kprep",
    )(proj, proj, proj, tab_a, tab_i, knw, iknw)


def _attn_kernel(lead, k_sel, tk, tkw, q_ref, iq_ref, iw_ref, az_ref, ta_ref, ti_ref, qnw_ref,
                 kn_ref, vb_ref, iklo_ref, ikhi_ref, o_ref, key_ref, iwb_ref, s_ref, qs_ref):
    f32 = jnp.float32
    tq = ROW_BLOCK
    sb = tk // tq
    ts = sb * tq
    nlb = tk // LANES
    qi = pl.program_id(1)
    si = qi // sb

    @pl.when(qi % sb == 0)
    def _():
        nch = si + 1
        ti = ti_ref[...]
        npair = IDX_HEADS // 2
        xi = jnp.concatenate([iq_ref[:, p * LANES:(p + 1) * LANES] for p in range(npair)], axis=0)
        roti = _dot2(xi, _rot_half_matrix(IDX_DIM))
        cos_i, sin_i = ti[:, 0:LANES], ti[:, 2 * LANES:3 * LANES] - ti[:, LANES:2 * LANES]
        iq_all = xi.reshape(npair, ts, LANES) * cos_i[None] + roti.reshape(npair, ts, LANES) * sin_i[None]
        iq_all = iq_all.astype(MM).reshape(npair * ts, LANES)
        iw = iw_ref[...] * (IDX_HEADS ** -0.5 * IDX_DIM ** -0.5)
        for h in range(IDX_HEADS):
            iwb_ref[h] = jnp.broadcast_to(iw[:, h:h + 1], (ts, LANES))

        qrow = si * ts + lax.broadcasted_iota(jnp.int32, (ts, 1), 0)
        kcol0 = lax.broadcasted_iota(jnp.int32, (1, tk), 1)

        def score_chunk(j, carry):
            off = pl.multiple_of(j * tk, tk)
            ik2 = jnp.concatenate([iklo_ref[pl.ds(off, tk), :], ikhi_ref[pl.ds(off, tk), :]], axis=0)
            cols = [None] * nlb
            for half in range(2):
                p0 = half * (npair // 2)
                l_both = _dot_nt(iq_all[p0 * ts:(p0 + npair // 2) * ts], ik2)
                for c in range(nlb):
                    le = slice(c * LANES, (c + 1) * LANES)
                    lo = slice(tk + c * LANES, tk + (c + 1) * LANES)
                    for pp in range(npair // 2):
                        rs = slice(pp * ts, (pp + 1) * ts)
                        p = p0 + pp
                        term = (iwb_ref[2 * p] * jnp.maximum(l_both[rs, le], 0.0)
                                + iwb_ref[2 * p + 1] * jnp.maximum(l_both[rs, lo], 0.0))
                        cols[c] = term if cols[c] is None else cols[c] + term
            sc = jnp.concatenate(cols, axis=1)
            bits = pltpu.bitcast(sc, jnp.int32)
            keys = bits ^ ((bits >> 31) & jnp.int32(0x7FFFFFFF))
            kcol = off + kcol0
            visible = (kcol <= qrow) & (kcol >= lead)
            key_ref[:, pl.ds(off, tk)] = jnp.where(visible, keys, jnp.int32(INT_MIN))
            return carry

        lax.fori_loop(0, nch, score_chunk, 0)

        ones_cnt = jnp.ones((LANES, LANES), jnp.bfloat16)

        def bit_step(bi, t):
            cand_s = t ^ lax.shift_left(jnp.int32(1), 31 - bi)
            accs = []
            for r in range(sb):
                rs = slice(r * tq, (r + 1) * tq)
                cand_r = cand_s[rs]

                def count_chunk(j, acc, rs=rs, cand_r=cand_r):
                    off = pl.multiple_of(j * tk, tk)
                    for c in range(nlb):
                        kk = key_ref[rs, pl.ds(pl.multiple_of(off + c * LANES, LANES), LANES)]
                        acc = acc + jnp.where(kk >= cand_r, 1.0, 0.0)
                    return acc

                accs.append(lax.fori_loop(0, nch, count_chunk, jnp.zeros((tq, LANES), f32)))
            acc = jnp.concatenate(accs, axis=0).astype(jnp.bfloat16)
            cnt = jnp.dot(acc, ones_cnt, preferred_element_type=f32)
            return jnp.where(cnt >= float(k_sel), cand_s, t)

        t_s = lax.fori_loop(0, 32, bit_step, jnp.full((ts, LANES), INT_MIN, jnp.int32))
        thr = jnp.maximum(t_s, jnp.int32(INT_MIN + 1))

        def bias_chunk(j, carry):
            off = pl.multiple_of(j * tk, tk)
            for c in range(nlb):
                cols = pl.ds(pl.multiple_of(off + c * LANES, LANES), LANES)
                kk = key_ref[:, cols]
                key_ref[:, cols] = pltpu.bitcast(jnp.where(kk >= thr, 0.0, NEG_BIG), jnp.int32)
            return carry

        lax.fori_loop(0, nch, bias_chunk, 0)

        ta = ta_ref[...]
        cos_a, sin_a = ta[:, 0:LANES], ta[:, 2 * LANES:3 * LANES] - ta[:, LANES:2 * LANES]
        rot_m = _rot_half_matrix(ATTN_HEAD_DIM)
        scale = ATTN_HEAD_DIM ** -0.5 * LOG2E
        for g in range(ATTN_KV_HEADS):
            heads = range(g * KV_GROUP, (g + 1) * KV_GROUP)
            xs = jnp.concatenate([q_ref[:, h * ATTN_HEAD_DIM:(h + 1) * ATTN_HEAD_DIM] for h in heads], axis=0)
            ms = _dot2(xs * xs, ones_cnt) * (1.0 / ATTN_HEAD_DIM)
            xn = xs * lax.rsqrt(ms + EPS) * qnw_ref[...]
            rot = _dot2(xn, rot_m)
            qr = xn.reshape(KV_GROUP, ts, LANES) * cos_a[None] + rot.reshape(KV_GROUP, ts, LANES) * sin_a[None]
            qr = (qr * scale).astype(MM)
            for hh in range(KV_GROUP):
                for r in range(sb):
                    row0 = (g * KV_GROUP + hh) * tq
                    qs_ref[r, row0:row0 + tq, :] = qr[hh, r * tq:(r + 1) * tq, :]

    rows4 = KV_GROUP * tq
    brow = pl.multiple_of((qi % sb) * tq, tq)
    macc0 = jnp.full((rows4, LANES), NEG_BIG, f32)
    acc0 = jnp.zeros((rows4, 2 * ATTN_HEAD_DIM), f32)

    def logits_chunk(g, off, w, macc):
        cs = slice(g * ATTN_HEAD_DIM, (g + 1) * ATTN_HEAD_DIM)
        q4 = qs_ref[qi % sb, g * rows4:(g + 1) * rows4, :]
        s = _dot_nt(q4, kn_ref[pl.ds(off, w), cs])
        bias = pltpu.bitcast(key_ref[pl.ds(brow, tq), pl.ds(off, w)], f32)
        s = (s.reshape(KV_GROUP, tq, w) + bias[None]).reshape(rows4, w)
        s_ref[g % 2, :, pl.ds(off, w)] = s
        for c in range(w // LANES):
            macc = jnp.maximum(macc, s[:, c * LANES:(c + 1) * LANES])
        return macc

    def pv_chunk(g, m, off, w, acc):
        cs = slice(g * ATTN_HEAD_DIM, (g + 1) * ATTN_HEAD_DIM)
        p = jnp.exp2(s_ref[g % 2, :, pl.ds(off, w)] - m).astype(MM)
        v1 = jnp.concatenate([vb_ref[pl.ds(off, w), cs], jnp.ones((w, ATTN_HEAD_DIM), MM)], axis=1)
        return acc + jnp.dot(p, v1, preferred_element_type=f32)

    needed = qi * tq + tq
    nwide = needed // tkw
    nnarrow = (needed - nwide * tkw + tk - 1) // tk

    def sweep(fn, carry):
        carry = lax.fori_loop(0, nwide, lambda j, c: fn(pl.multiple_of(j * tkw, tkw), tkw, c), carry)
        return lax.fori_loop(
            0, nnarrow, lambda j, c: fn(pl.multiple_of(nwide * tkw + j * tk, tk), tk, c), carry)

    macc = sweep(functools.partial(logits_chunk, 0), macc0)
    for g in range(ATTN_KV_HEADS):
        m = jnp.max(macc, axis=-1, keepdims=True)
        if g + 1 < ATTN_KV_HEADS:
            def both(off, w, carry, g=g, m=m):
                return pv_chunk(g, m, off, w, carry[0]), logits_chunk(g + 1, off, w, carry[1])
            acc, macc = sweep(both, (acc0, macc0))
        else:
            acc = sweep(functools.partial(pv_chunk, g, m), acc0)
        o = acc[:, :ATTN_HEAD_DIM] / acc[:, ATTN_HEAD_DIM:]
        for r in range(KV_GROUP):
            h = g * KV_GROUP + r
            hs = slice(h * ATTN_HEAD_DIM, (h + 1) * ATTN_HEAD_DIM)
            o_ref[:, hs] = (o[r * tq:(r + 1) * tq, :] * _silu(az_ref[:, hs])).astype(o_ref.dtype)


def _attn(proj, kn, vb, iklo, ikhi, tab_a, tab_i, qnw, lead, k_sel, bsz, tp):
    tq = ROW_BLOCK
    nq = tp // tq
    sb = _pick_tile(nq, 3, 1)
    tk = sb * tq
    tkw = ATTN_WIDE_CHUNKS * tk
    ns = nq // sb

    def seg(name):
        w = _DST[name][1]
        cbk = _col_block(name)
        return pl.BlockSpec((tq, w), lambda b, i: (b * nq + i, cbk))

    def seg_sb(name):
        w = _DST[name][1]
        cbk = _col_block(name)
        return pl.BlockSpec((tk, w), lambda b, i: (b * ns + i // sb, cbk), pipeline_mode=pl.Buffered(1))

    tab_sb = pl.BlockSpec((tk, 3 * LANES), lambda b, i: (i // sb, 0), pipeline_mode=pl.Buffered(1))
    vec = pl.BlockSpec((1, LANES), lambda b, i: (0, 0))
    keys = lambda w: pl.BlockSpec((tp, w), lambda b, i: (b, 0), pipeline_mode=pl.Buffered(1))
    return pl.pallas_call(
        functools.partial(_attn_kernel, lead, k_sel, tk, tkw),
        grid=(bsz, nq),
        in_specs=[seg_sb("q"), seg_sb("iq"), seg_sb("iw"), seg("az"), tab_sb, tab_sb, vec,
                  keys(KV_WIDTH), keys(KV_WIDTH), keys(LANES), keys(LANES)],
        out_specs=pl.BlockSpec((tq, ATTN_WIDTH), lambda b, i: (b * nq + i, 0)),
        out_shape=jax.ShapeDtypeStruct((bsz * tp, ATTN_WIDTH), MM),
        scratch_shapes=[pltpu.VMEM((tk, tp), jnp.int32),
                        pltpu.VMEM((IDX_HEADS, tk, LANES), jnp.float32),
                        pltpu.VMEM((2, KV_GROUP * tq, tp), jnp.float32),
                        pltpu.VMEM((sb, ATTN_HEADS * tq, ATTN_HEAD_DIM), MM)],
        compiler_params=_params(("parallel", "arbitrary")),
        name="attn",
    )(proj, proj, proj, proj, tab_a, tab_i, qnw, kn, vb, iklo, ikhi)


def _merge_kernel(l_ref, ys_ref, ya_ref, gs_ref, ga_ref, ws_ref, wa_ref, o_ref):
    del l_ref
    ys = jnp.dot(ys_ref[...], ws_ref[...], preferred_element_type=jnp.float32)
    ya = jnp.dot(ya_ref[...], wa_ref[...], preferred_element_type=jnp.float32)
    o_ref[...] = (_sigmoid(gs_ref[...]) * ys + _sigmoid(ga_ref[...]) * ya).astype(o_ref.dtype)


def _merge(l_arr, ys, ya, proj, ws, wa):
    m = ys.shape[0]
    tm = _pick_tile(m, 1056, 8)
    tn = 512
    gsb = _DST["gs"][0] // tn
    gab = _DST["ga"][0] // tn
    return pl.pallas_call(
        _merge_kernel,
        grid_spec=pltpu.PrefetchScalarGridSpec(
            num_scalar_prefetch=1, grid=(m // tm, D_MODEL // tn),
            in_specs=[pl.BlockSpec((tm, D_INNER), lambda i, j, l: (i, 0)),
                      pl.BlockSpec((tm, ATTN_WIDTH), lambda i, j, l: (i, 0)),
                      pl.BlockSpec((tm, tn), lambda i, j, l: (i, gsb + j)),
                      pl.BlockSpec((tm, tn), lambda i, j, l: (i, gab + j)),
                      pl.BlockSpec((None, D_INNER, tn), lambda i, j, l: (l[0], 0, j)),
                      pl.BlockSpec((None, ATTN_WIDTH, tn), lambda i, j, l: (l[0], 0, j))],
            out_specs=pl.BlockSpec((tm, tn), lambda i, j, l: (i, j))),
        out_shape=jax.ShapeDtypeStruct((m, D_MODEL), MM),
        compiler_params=_params(("parallel", "arbitrary")),
        name="merge",
    )(l_arr, ys, ya, proj, proj, ws, wa)


def _outproj_kernel(lead, tp, l_ref, mg_ref, h_ref, w_ref, o_ref):
    del l_ref
    tm = mg_ref.shape[0]
    upd = jnp.dot(mg_ref[...], w_ref[...], preferred_element_type=jnp.float32)
    rows = pl.program_id(0) * tm + lax.broadcasted_iota(jnp.int32, (tm, 1), 0)
    o_ref[...] = jnp.where(rows % tp >= lead, h_ref[...] + upd, 0.0)


def _outproj(l_arr, mg, h2, wo, lead, tp):
    m = mg.shape[0]
    tm = _pick_tile(m, 1056, 8)
    tn = 512
    return pl.pallas_call(
        functools.partial(_outproj_kernel, lead, tp),
        grid_spec=pltpu.PrefetchScalarGridSpec(
            num_scalar_prefetch=1, grid=(m // tm, D_MODEL // tn),
            in_specs=[pl.BlockSpec((tm, D_MODEL), lambda i, j, l: (i, 0)),
                      pl.BlockSpec((tm, tn), lambda i, j, l: (i, j)),
                      pl.BlockSpec((None, D_MODEL, tn), lambda i, j, l: (l[0], 0, j))],
            out_specs=pl.BlockSpec((tm, tn), lambda i, j, l: (i, j))),
        out_shape=jax.ShapeDtypeStruct((m, D_MODEL), jnp.float32),
        input_output_aliases={2: 0},
        compiler_params=_params(("parallel", "arbitrary")),
        name="outproj",
    )(l_arr, mg, h2, wo)


def _rope_table(pos, head_dim):
    rot = head_dim // ROPE_FRACTION
    half = rot // 2
    inv = ROPE_THETA ** (-jnp.arange(0, rot, 2, dtype=jnp.float32) / rot)
    ang = pos.astype(jnp.float32)[:, None] * inv[None, :]
    cos, sin = jnp.cos(ang), jnp.sin(ang)
    n = pos.shape[0]
    rest = head_dim - rot
    c = jnp.concatenate([cos, cos, jnp.ones((n, rest), jnp.float32)], axis=1)
    s_a = jnp.concatenate([-sin, jnp.zeros((n, head_dim - half), jnp.float32)], axis=1)
    s_b = jnp.concatenate([jnp.zeros((n, half), jnp.float32), sin, jnp.zeros((n, rest), jnp.float32)], axis=1)
    rep = LANES // head_dim
    return jnp.concatenate([jnp.tile(c, (1, rep)), jnp.tile(s_a, (1, rep)), jnp.tile(s_b, (1, rep))], axis=1)


def _pad_lanes(a, width=LANES):
    return jnp.pad(a, [(0, 0)] * (a.ndim - 1) + [(0, width - a.shape[-1])])


def _wprep_kernel(w_ref, o_ref):
    rows = w_ref.shape[0]
    lane = lax.broadcasted_iota(jnp.int32, (rows, LANES), 1)
    for name, (d0, dw) in _DST.items():
        s0, sw = _SRC[name]
        if sw == dw:
            o_ref[:, d0:d0 + dw] = w_ref[:, s0:s0 + dw].astype(o_ref.dtype)
        else:
            blk = w_ref[:, s0:s0 + dw]
            o_ref[:, d0:d0 + dw] = jnp.where(lane < sw, blk, 0.0).astype(o_ref.dtype)
    used = sum(w for _, w in _DST.values())
    o_ref[:, used:NP] = jnp.zeros((rows, NP - used), o_ref.dtype)


def _prep_w_in(w_in):
    depth, d, n_in = w_in.shape
    rows = _pick_tile(d, 128, 8)
    return pl.pallas_call(
        _wprep_kernel,
        grid=(depth, d // rows),
        in_specs=[pl.BlockSpec((None, rows, n_in), lambda l, i: (l, i, 0))],
        out_specs=pl.BlockSpec((None, rows, NP), lambda l, i: (l, i, 0)),
        out_shape=jax.ShapeDtypeStruct((depth, d, NP), MM),
        compiler_params=_params(("parallel", "parallel")),
        name="wprep",
    )(w_in)


def kernel(x, meta_tokens, norm_w, w_in, conv_w, conv_b, dt_bias, a_log, d_skip, ssm_norm_w,
           w_ssm_out, q_norm_w, k_norm_w, idx_k_norm_w, w_attn_out, w_out):
    bsz, seq, d = x.shape
    depth = norm_w.shape[0]
    assert d == D_MODEL and w_in.shape[-1] == N_IN
    t = seq + N_META
    lead = (-t) % ROW_BLOCK
    tp = t + lead
    k_sel = min(TOPK_MAX, seq // 4)

    meta = jnp.broadcast_to(meta_tokens[None].astype(x.dtype), (bsz, N_META, d))
    h = jnp.concatenate([jnp.zeros((bsz, lead, d), x.dtype), meta, x], axis=1).reshape(bsz * tp, d)

    pos = jnp.arange(tp) - lead
    tab_a = _rope_table(pos, ATTN_HEAD_DIM)
    tab_i = _rope_table(pos, IDX_DIM)

    wp = _prep_w_in(w_in)
    ws = w_ssm_out.astype(MM)
    wa = w_attn_out.astype(MM)
    wo = w_out.astype(MM)
    cw_x, cw_b, cw_c = (conv_w[:, :, :D_INNER], conv_w[:, :, D_INNER:D_INNER + BC_WIDTH],
                        conv_w[:, :, D_INNER + BC_WIDTH:])
    cb_x, cb_b, cb_c = (conv_b[:, None, :D_INNER], conv_b[:, None, D_INNER:D_INNER + BC_WIDTH],
                        conv_b[:, None, D_INNER + BC_WIDTH:])
    dtb = _pad_lanes(dt_bias)[:, None, :]
    alog = _pad_lanes(a_log)[:, None, :]
    dsk = jnp.repeat(d_skip, SSM_HEAD_DIM, axis=-1)[:, None, :]
    snw = ssm_norm_w[:, None, :]
    iknw = _pad_lanes(idx_k_norm_w)[:, None, :]
    qnw = q_norm_w[:, None, :]
    knw = k_norm_w[:, None, :]
    nw = norm_w[:, None, :]

    def layer(l, h):
        l_arr = jnp.reshape(l, (1,)).astype(jnp.int32)
        proj = _inproj(l_arr, h, nw[l], wp)
        ys = _ssd(proj, lead, bsz, tp, cw_x[l], cw_b[l], cw_c[l], cb_x[l], cb_b[l], cb_c[l],
                  dtb[l], alog[l], dsk[l], snw[l])
        kn, vb, iklo, ikhi = _kprep(proj, tab_a, tab_i, knw[l], iknw[l], tp)
        ya = _attn(proj, kn, vb, iklo, ikhi, tab_a, tab_i, qnw[l], lead, k_sel, bsz, tp)
        mg = _merge(l_arr, ys, ya, proj, ws, wa)
        return _outproj(l_arr, mg, h, wo, lead, tp)

    h = lax.fori_loop(0, depth, layer, h)
    return h.reshape(bsz, tp, d)[:, lead + N_META:]
```

```python
import functools

import jax
import jax.numpy as jnp
from jax import lax
from jax.experimental import pallas as pl
from jax.experimental.pallas import tpu as pltpu

D_MODEL = 2048
N_META = 16
D_INNER = 2 * D_MODEL
SSM_HEAD_DIM = 64
SSM_HEADS = D_INNER // SSM_HEAD_DIM
SSM_GROUPS = 8
SSM_STATE = 128
CONV_WIDTH = 4
BC_WIDTH = SSM_GROUPS * SSM_STATE
GROUP_WIDTH = D_INNER // SSM_GROUPS
ATTN_HEADS = 16
ATTN_KV_HEADS = 4
ATTN_HEAD_DIM = 128
ATTN_WIDTH = ATTN_HEADS * ATTN_HEAD_DIM
KV_WIDTH = ATTN_KV_HEADS * ATTN_HEAD_DIM
KV_GROUP = ATTN_HEADS // ATTN_KV_HEADS
IDX_HEADS = 16
IDX_DIM = 64
IDX_WIDTH = IDX_HEADS * IDX_DIM
TOPK_MAX = 256
ROPE_THETA = 500000.0
ROPE_FRACTION = 4
EPS = 1e-6

LANES = 128
ROW_BLOCK = 128
ATTN_WIDE_CHUNKS = 4
VMEM_LIMIT = 56 * 1024 * 1024

MM = jnp.bfloat16
NEG_BIG = -1e30
LOG2E = 1.4426950408889634
INT_MIN = -2 ** 31

_SRC = {}
_o = 0
for _n, _w in (("z", D_INNER), ("x", D_INNER), ("b", BC_WIDTH), ("c", BC_WIDTH), ("dt", SSM_HEADS),
               ("q", ATTN_WIDTH), ("k", KV_WIDTH), ("v", KV_WIDTH), ("az", ATTN_WIDTH),
               ("iq", IDX_WIDTH), ("ik", IDX_DIM), ("iw", IDX_HEADS), ("gs", D_MODEL), ("ga", D_MODEL)):
    _SRC[_n] = (_o, _w)
    _o += _w
N_IN = _o

_DST = {}
_o = 0
for _n, _w in (("z", 4096), ("x", 4096), ("q", 2048), ("az", 2048), ("gs", 2048), ("ga", 2048),
               ("b", 1024), ("c", 1024), ("iq", 1024), ("k", 512), ("v", 512),
               ("dt", 128), ("ik", 128), ("iw", 128)):
    assert _o % _w == 0
    _DST[_n] = (_o, _w)
    _o += _w
PROJ_TILE_N = 512
NP = -(-_o // PROJ_TILE_N) * PROJ_TILE_N


def _col_block(name):
    off, w = _DST[name]
    return off // w


def _pick_tile(n, target, mult):
    best = None
    for t in range(mult, min(n, target) + 1, mult):
        if n % t == 0:
            best = t
    assert best is not None, (n, target, mult)
    return best


def _sigmoid(v):
    return 0.5 + 0.5 * jnp.tanh(0.5 * v)


def _silu(v):
    h = 0.5 * v
    return h + h * jnp.tanh(h)


def _dot(a, b):
    return jnp.dot(a.astype(MM), b.astype(MM), preferred_element_type=jnp.float32)


def _dot_nt(a, b):
    return lax.dot_general(a.astype(MM), b.astype(MM), (((1,), (1,)), ((), ())),
                           preferred_element_type=jnp.float32)


def _dot_tn(a, b):
    return lax.dot_general(a.astype(MM), b.astype(MM), (((0,), (0,)), ((), ())),
                           preferred_element_type=jnp.float32)


def _params(sem):
    return pltpu.CompilerParams(dimension_semantics=sem, vmem_limit_bytes=VMEM_LIMIT)


def _inproj_kernel(l_ref, h_ref, nw_ref, w_ref, o_ref, hn_ref):
    del l_ref

    @pl.when(pl.program_id(1) == 0)
    def _():
        x = h_ref[...]
        ms = jnp.mean(x * x, axis=-1, keepdims=True)
        hn_ref[...] = (x * lax.rsqrt(ms + EPS) * nw_ref[...]).astype(hn_ref.dtype)

    o_ref[...] = jnp.dot(hn_ref[...], w_ref[...], preferred_element_type=jnp.float32)


def _inproj(l_arr, h2, nw, wp):
    m, d = h2.shape
    tm = _pick_tile(m, 1056, 8)
    tn = PROJ_TILE_N
    return pl.pallas_call(
        _inproj_kernel,
        grid_spec=pltpu.PrefetchScalarGridSpec(
            num_scalar_prefetch=1, grid=(m // tm, NP // tn),
            in_specs=[pl.BlockSpec((tm, d), lambda i, j, l: (i, 0)),
                      pl.BlockSpec((1, d), lambda i, j, l: (0, 0)),
                      pl.BlockSpec((None, d, tn), lambda i, j, l: (l[0], 0, j))],
            out_specs=pl.BlockSpec((tm, tn), lambda i, j, l: (i, j)),
            scratch_shapes=[pltpu.VMEM((tm, d), MM)]),
        out_shape=jax.ShapeDtypeStruct((m, NP), jnp.float32),
        compiler_params=_params(("parallel", "arbitrary")),
        name="inproj",
    )(l_arr, h2, nw, wp)


def _split3(a):
    hi = a.astype(jnp.bfloat16)
    r1 = a - hi.astype(jnp.float32)
    mid = r1.astype(jnp.bfloat16)
    lo = (r1 - mid.astype(jnp.float32)).astype(jnp.bfloat16)
    return hi, mid, lo


def _expand_heads(a, e):
    hi, mid, lo = _split3(a)
    f = functools.partial(jnp.dot, preferred_element_type=jnp.float32)
    return f(hi, e) + f(mid, e) + f(lo, e)


def _ssd_kernel(lead, z_ref, x_ref, b_ref, c_ref, dt_ref, cwx_ref, cwb_ref, cwc_ref,
                cbx_ref, cbb_ref, cbc_ref, dtb_ref, alog_ref, dsk_ref, nw_ref,
                o_ref, extx, extb, extc, state, ybuf, e_ref):
    f32 = jnp.float32
    ci = pl.program_id(1)
    q = ROW_BLOCK

    @pl.when(ci == 0)
    def _():
        extx[0:8, :] = jnp.zeros((8, D_INNER), f32)
        extb[0:8, :] = jnp.zeros((8, BC_WIDTH), f32)
        extc[0:8, :] = jnp.zeros((8, BC_WIDTH), f32)
        state[...] = jnp.zeros(state.shape, f32)
        hrow = lax.broadcasted_iota(jnp.int32, (LANES, D_INNER), 0)
        hcol = lax.broadcasted_iota(jnp.int32, (LANES, D_INNER), 1) // SSM_HEAD_DIM
        e_ref[...] = (hrow == hcol).astype(jnp.bfloat16)

    def conv_silu(u_ref, ext, w_ref, bias_ref):
        ext[8:8 + q, :] = u_ref[...]
        acc = bias_ref[...] + w_ref[CONV_WIDTH - 1:CONV_WIDTH, :] * ext[8:8 + q, :]
        for k in range(CONV_WIDTH - 1):
            s = 8 - (CONV_WIDTH - 1) + k
            acc = acc + w_ref[k:k + 1, :] * ext[s:s + q, :]
        ext[0:8, :] = ext[q:q + 8, :]
        return _silu(acc)

    rows = ci * q + lax.broadcasted_iota(jnp.int32, (q, 1), 0)
    valid = rows >= lead

    xa = jnp.where(valid, conv_silu(x_ref, extx, cwx_ref, cbx_ref), 0.0)
    ba = conv_silu(b_ref, extb, cwb_ref, cbb_ref)
    ca = conv_silu(c_ref, extc, cwc_ref, cbc_ref)

    dtr = dt_ref[...] + dtb_ref[...]
    dt = jnp.maximum(dtr, 0.0) + jnp.log(1.0 + jnp.exp(-jnp.abs(dtr)))
    dt = jnp.where(valid, dt, 0.0)
    a = -jnp.exp(alog_ref[...])
    ri = lax.broadcasted_iota(jnp.int32, (q, q), 0)
    cj = lax.broadcasted_iota(jnp.int32, (q, q), 1)
    causal = ri >= cj
    tri = causal.astype(f32)
    acs = jnp.dot(tri, dt * a, preferred_element_type=f32, precision=lax.Precision.HIGHEST)
    acs_t = acs.T
    eacs = jnp.exp(acs)
    dte = jnp.exp(acs[q - 1:q, :] - acs)

    e = e_ref[...]
    dt_x = _expand_heads(dt, e)
    eacs_x = _expand_heads(eacs, e)
    dte_x = _expand_heads(dte, e)

    xdt = xa * dt_x
    xd_end = (xdt * dte_x).astype(MM)
    xdt_m = xdt.astype(MM)
    lane = lax.broadcasted_iota(jnp.int32, (q, LANES), 1)
    lo_half = lane < SSM_HEAD_DIM

    heads_per_group = SSM_HEADS // SSM_GROUPS
    for g in range(SSM_GROUPS):
        gs = slice(g * GROUP_WIDTH, (g + 1) * GROUP_WIDTH)
        bg = ba[:, g * SSM_STATE:(g + 1) * SSM_STATE]
        cg = ca[:, g * SSM_STATE:(g + 1) * SSM_STATE]
        cb = _dot_nt(cg, bg)
        h_in = state[g]
        y_off = _dot(cg, h_in) * eacs_x[:, gs]
        s_new = _dot_tn(bg, xd_end[:, gs])
        state[g] = h_in * eacs_x[q - 1:q, gs] + s_new
        for pr in range(heads_per_group // 2):
            h0 = g * heads_per_group + 2 * pr
            ms = []
            for hh in (h0, h0 + 1):
                diff = acs[:, hh:hh + 1] - acs_t[hh:hh + 1, :]
                lmat = jnp.exp(jnp.where(causal, diff, -jnp.inf))
                ms.append((cb * lmat).astype(MM))
            lhs = jnp.concatenate(ms, axis=1)
            xp = xdt_m[:, h0 * SSM_HEAD_DIM:(h0 + 2) * SSM_HEAD_DIM]
            zero = jnp.zeros_like(xp)
            rhs = jnp.concatenate([jnp.where(lo_half, xp, zero), jnp.where(lo_half, zero, xp)], axis=0)
            yd = jnp.dot(lhs, rhs, preferred_element_type=f32)
            cs = slice(h0 * SSM_HEAD_DIM, (h0 + 2) * SSM_HEAD_DIM)
            ybuf[:, cs] = yd + y_off[:, 2 * pr * SSM_HEAD_DIM:(2 * pr + 2) * SSM_HEAD_DIM]

    y = ybuf[...] + dsk_ref[...] * xa
    gz = y * _silu(z_ref[...])
    for g in range(SSM_GROUPS):
        gs = slice(g * GROUP_WIDTH, (g + 1) * GROUP_WIDTH)
        gg = gz[:, gs]
        ms = jnp.mean(gg * gg, axis=-1, keepdims=True)
        o_ref[:, gs] = (gg * lax.rsqrt(ms + EPS) * nw_ref[:, gs]).astype(o_ref.dtype)


def _ssd(proj, lead, bsz, tp, cwx, cwb, cwc, cbx, cbb, cbc, dtb, alog, dsk, nw):
    nc = tp // ROW_BLOCK
    q = ROW_BLOCK

    def seg(name):
        w = _DST[name][1]
        cbk = _col_block(name)
        return pl.BlockSpec((q, w), lambda b, c: (b * nc + c, cbk))

    def full(a):
        return pl.BlockSpec(a.shape, lambda b, c: (0, 0))

    small = (cwx, cwb, cwc, cbx, cbb, cbc, dtb, alog, dsk, nw)
    return pl.pallas_call(
        functools.partial(_ssd_kernel, lead),
        grid=(bsz, nc),
        in_specs=[seg("z"), seg("x"), seg("b"), seg("c"), seg("dt")] + [full(a) for a in small],
        out_specs=pl.BlockSpec((q, D_INNER), lambda b, c: (b * nc + c, 0)),
        out_shape=jax.ShapeDtypeStruct((bsz * tp, D_INNER), MM),
        scratch_shapes=[pltpu.VMEM((q + 8, D_INNER), jnp.float32),
                        pltpu.VMEM((q + 8, BC_WIDTH), jnp.float32),
                        pltpu.VMEM((q + 8, BC_WIDTH), jnp.float32),
                        pltpu.VMEM((SSM_GROUPS, SSM_STATE, GROUP_WIDTH), jnp.float32),
                        pltpu.VMEM((q, D_INNER), jnp.float32),
                        pltpu.VMEM((LANES, D_INNER), jnp.bfloat16)],
        compiler_params=_params(("parallel", "arbitrary")),
        name="ssd",
    )(proj, proj, proj, proj, proj, *small)


def _split2(a):
    hi = a.astype(jnp.bfloat16)
    lo = (a - hi.astype(jnp.float32)).astype(jnp.bfloat16)
    return hi, lo


def _dot2(a, b):
    hi, lo = _split2(a)
    f = functools.partial(jnp.dot, preferred_element_type=jnp.float32)
    return f(hi, b) + f(lo, b)


def _rot_half_matrix(head_dim):
    half = head_dim // ROPE_FRACTION // 2
    rj = lax.broadcasted_iota(jnp.int32, (LANES, LANES), 0)
    ci = lax.broadcasted_iota(jnp.int32, (LANES, LANES), 1)
    cm = ci % head_dim
    neg = (cm < half) & (rj == ci + half)
    pos = (cm >= half) & (cm < 2 * half) & (rj == ci - half)
    return jnp.where(neg, -1.0, jnp.where(pos, 1.0, 0.0)).astype(jnp.bfloat16)


def _rope128(x, tab, half):
    cos = tab[:, 0:LANES]
    s_a = tab[:, LANES:2 * LANES]
    s_b = tab[:, 2 * LANES:3 * LANES]
    return x * cos + pltpu.roll(x, LANES - half, 1) * s_a + pltpu.roll(x, half, 1) * s_b


def _rms128(x, w, width):
    ms = jnp.sum(x * x, axis=-1, keepdims=True) * (1.0 / width)
    return x * lax.rsqrt(ms + EPS) * w


def _kprep_kernel(k_ref, v_ref, ik_ref, ta_ref, ti_ref, knw_ref, iknw_ref,
                  ko_ref, vo_ref, iklo_ref, ikhi_ref):
    ta = ta_ref[...]
    for h in range(ATTN_KV_HEADS):
        cs = slice(h * ATTN_HEAD_DIM, (h + 1) * ATTN_HEAD_DIM)
        kn = _rms128(k_ref[:, cs], knw_ref[...], ATTN_HEAD_DIM)
        ko_ref[:, cs] = _rope128(kn, ta, ATTN_HEAD_DIM // ROPE_FRACTION // 2).astype(ko_ref.dtype)
    vo_ref[...] = v_ref[...].astype(vo_ref.dtype)
    ikn = _rms128(ik_ref[...], iknw_ref[...], IDX_DIM)
    ikn = _rope128(ikn, ti_ref[...], IDX_DIM // ROPE_FRACTION // 2)
    iklo_ref[...] = ikn.astype(iklo_ref.dtype)
    ikhi_ref[...] = pltpu.roll(ikn, IDX_DIM, 1).astype(ikhi_ref.dtype)


def _kprep(proj, tab_a, tab_i, knw, iknw, tp):
    m = proj.shape[0]
    tr = _pick_tile(tp, 1408, 8)
    nt = tp // tr

    def seg(name):
        w = _DST[name][1]
        cbk = _col_block(name)
        return pl.BlockSpec((tr, w), lambda i: (i, cbk))

    tab = pl.BlockSpec((tr, 3 * LANES), lambda i: (i % nt, 0))
    vec = pl.BlockSpec((1, LANES), lambda i: (0, 0))
    row = lambda w: pl.BlockSpec((tr, w), lambda i: (i, 0))
    return pl.pallas_call(
        _kprep_kernel,
        grid=(m // tr,),
        in_specs=[seg("k"), seg("v"), seg("ik"), tab, tab, vec, vec],
        out_specs=[row(KV_WIDTH), row(KV_WIDTH), row(LANES), row(LANES)],
        out_shape=[jax.ShapeDtypeStruct((m, KV_WIDTH), MM), jax.ShapeDtypeStruct((m, KV_WIDTH), MM),
                   jax.ShapeDtypeStruct((m, LANES), MM), jax.ShapeDtypeStruct((m, LANES), MM)],
        compiler_params=_params(("parallel",)),
        name="kprep",
    )(proj, proj, proj, tab_a, tab_i, knw, iknw)


def _attn_kernel(lead, k_sel, tk, tkw, q_ref, iq_ref, iw_ref, az_ref, ta_ref, ti_ref, qnw_ref,
                 kn_ref, vb_ref, iklo_ref, ikhi_ref, o_ref, key_ref, iwb_ref, s_ref, qs_ref):
    f32 = jnp.float32
    tq = ROW_BLOCK
    sb = tk // tq
    ts = sb * tq
    nlb = tk // LANES
    qi = pl.program_id(1)
    si = qi // sb

    @pl.when(qi % sb == 0)
    def _():
        nch = si + 1
        ti = ti_ref[...]
        npair = IDX_HEADS // 2
        xi = jnp.concatenate([iq_ref[:, p * LANES:(p + 1) * LANES] for p in range(npair)], axis=0)
        roti = _dot2(xi, _rot_half_matrix(IDX_DIM))
        cos_i, sin_i = ti[:, 0:LANES], ti[:, 2 * LANES:3 * LANES] - ti[:, LANES:2 * LANES]
        iq_all = xi.reshape(npair, ts, LANES) * cos_i[None] + roti.reshape(npair, ts, LANES) * sin_i[None]
        iq_all = iq_all.astype(MM).reshape(npair * ts, LANES)
        iw = iw_ref[...] * (IDX_HEADS ** -0.5 * IDX_DIM ** -0.5)
        for h in range(IDX_HEADS):
            iwb_ref[h] = jnp.broadcast_to(iw[:, h:h + 1], (ts, LANES))

        qrow = si * ts + lax.broadcasted_iota(jnp.int32, (ts, 1), 0)
        kcol0 = lax.broadcasted_iota(jnp.int32, (1, tk), 1)

        def score_chunk(j, carry):
            off = pl.multiple_of(j * tk, tk)
            ik2 = jnp.concatenate([iklo_ref[pl.ds(off, tk), :], ikhi_ref[pl.ds(off, tk), :]], axis=0)
            cols = [None] * nlb
            for half in range(2):
                p0 = half * (npair // 2)
                l_both = _dot_nt(iq_all[p0 * ts:(p0 + npair // 2) * ts], ik2)
                for c in range(nlb):
                    le = slice(c * LANES, (c + 1) * LANES)
                    lo = slice(tk + c * LANES, tk + (c + 1) * LANES)
                    for pp in range(npair // 2):
                        rs = slice(pp * ts, (pp + 1) * ts)
                        p = p0 + pp
                        term = (iwb_ref[2 * p] * jnp.maximum(l_both[rs, le], 0.0)
                                + iwb_ref[2 * p + 1] * jnp.maximum(l_both[rs, lo], 0.0))
                        cols[c] = term if cols[c] is None else cols[c] + term
            sc = jnp.concatenate(cols, axis=1)
            bits = pltpu.bitcast(sc, jnp.int32)
            keys = bits ^ ((bits >> 31) & jnp.int32(0x7FFFFFFF))
            kcol = off + kcol0
            visible = (kcol <= qrow) & (kcol >= lead)
            key_ref[:, pl.ds(off, tk)] = jnp.where(visible, keys, jnp.int32(INT_MIN))
            return carry

        lax.fori_loop(0, nch, score_chunk, 0)

        ones_cnt = jnp.ones((LANES, LANES), jnp.bfloat16)

        def bit_step(bi, t):
            cand_s = t ^ lax.shift_left(jnp.int32(1), 31 - bi)
            accs = []
            for r in range(sb):
                rs = slice(r * tq, (r + 1) * tq)
                cand_r = cand_s[rs]

                def count_blocks(off, nblk, acc, rs=rs, cand_r=cand_r):
                    for c in range(nblk):
                        kk = key_ref[rs, pl.ds(pl.multiple_of(off + c * LANES, LANES), LANES)]
                        acc = acc + jnp.where(kk >= cand_r, 1.0, 0.0)
                    return acc

                acc = lax.fori_loop(0, nch - 1,
                                    lambda j, a, f=count_blocks: f(pl.multiple_of(j * tk, tk), nlb, a),
                                    jnp.zeros((tq, LANES), f32))
                accs.append(count_blocks(pl.multiple_of((nch - 1) * tk, tk), (r + 1) * (tq // LANES), acc))
            acc = jnp.concatenate(accs, axis=0).astype(jnp.bfloat16)
            cnt = jnp.dot(acc, ones_cnt, preferred_element_type=f32)
            return jnp.where(cnt >= float(k_sel), cand_s, t)

        t_s = lax.fori_loop(0, 32, bit_step, jnp.full((ts, LANES), INT_MIN, jnp.int32))
        thr = jnp.maximum(t_s, jnp.int32(INT_MIN + 1))

        def bias_chunk(j, carry):
            off = pl.multiple_of(j * tk, tk)
            for c in range(nlb):
                cols = pl.ds(pl.multiple_of(off + c * LANES, LANES), LANES)
                kk = key_ref[:, cols]
                key_ref[:, cols] = pltpu.bitcast(jnp.where(kk >= thr, 0.0, NEG_BIG), jnp.int32)
            return carry

        lax.fori_loop(0, nch, bias_chunk, 0)

        ta = ta_ref[...]
        cos_a, sin_a = ta[:, 0:LANES], ta[:, 2 * LANES:3 * LANES] - ta[:, LANES:2 * LANES]
        rot_m = _rot_half_matrix(ATTN_HEAD_DIM)
        scale = ATTN_HEAD_DIM ** -0.5 * LOG2E
        for g in range(ATTN_KV_HEADS):
            heads = range(g * KV_GROUP, (g + 1) * KV_GROUP)
            xs = jnp.concatenate([q_ref[:, h * ATTN_HEAD_DIM:(h + 1) * ATTN_HEAD_DIM] for h in heads], axis=0)
            ms = _dot2(xs * xs, ones_cnt) * (1.0 / ATTN_HEAD_DIM)
            xn = xs * lax.rsqrt(ms + EPS) * qnw_ref[...]
            rot = _dot2(xn, rot_m)
            qr = xn.reshape(KV_GROUP, ts, LANES) * cos_a[None] + rot.reshape(KV_GROUP, ts, LANES) * sin_a[None]
            qr = (qr * scale).astype(MM)
            for hh in range(KV_GROUP):
                for r in range(sb):
                    row0 = (g * KV_GROUP + hh) * tq
                    qs_ref[r, row0:row0 + tq, :] = qr[hh, r * tq:(r + 1) * tq, :]

    rows4 = KV_GROUP * tq
    brow = pl.multiple_of((qi % sb) * tq, tq)
    macc0 = jnp.full((rows4, LANES), NEG_BIG, f32)
    acc0 = jnp.zeros((rows4, 2 * ATTN_HEAD_DIM), f32)

    def logits_chunk(g, off, w, macc):
        cs = slice(g * ATTN_HEAD_DIM, (g + 1) * ATTN_HEAD_DIM)
        q4 = qs_ref[qi % sb, g * rows4:(g + 1) * rows4, :]
        s = _dot_nt(q4, kn_ref[pl.ds(off, w), cs])
        bias = pltpu.bitcast(key_ref[pl.ds(brow, tq), pl.ds(off, w)], f32)
        s = (s.reshape(KV_GROUP, tq, w) + bias[None]).reshape(rows4, w)
        s_ref[g % 2, :, pl.ds(off, w)] = s
        for c in range(w // LANES):
            macc = jnp.maximum(macc, s[:, c * LANES:(c + 1) * LANES])
        return macc

    def pv_chunk(g, m, off, w, acc):
        cs = slice(g * ATTN_HEAD_DIM, (g + 1) * ATTN_HEAD_DIM)
        p = jnp.exp2(s_ref[g % 2, :, pl.ds(off, w)] - m).astype(MM)
        v1 = jnp.concatenate([vb_ref[pl.ds(off, w), cs], jnp.ones((w, ATTN_HEAD_DIM), MM)], axis=1)
        return acc + jnp.dot(p, v1, preferred_element_type=f32)

    needed = qi * tq + tq
    tkm = tkw // 2
    nwide = needed // tkw
    nmid = (needed - nwide * tkw) // tkm
    base = nwide * tkw + nmid * tkm
    nnarrow = (needed - base + tk - 1) // tk

    def sweep(fn, carry):
        carry = lax.fori_loop(0, nwide, lambda j, c: fn(pl.multiple_of(j * tkw, tkw), tkw, c), carry)
        carry = lax.fori_loop(0, nmid, lambda j, c: fn(pl.multiple_of(nwide * tkw, tkm), tkm, c), carry)
        return lax.fori_loop(0, nnarrow, lambda j, c: fn(pl.multiple_of(base + j * tk, tk), tk, c), carry)

    macc = sweep(functools.partial(logits_chunk, 0), macc0)
    for g in range(ATTN_KV_HEADS):
        m = jnp.max(macc, axis=-1, keepdims=True)
        if g + 1 < ATTN_KV_HEADS:
            def both(off, w, carry, g=g, m=m):
                return pv_chunk(g, m, off, w, carry[0]), logits_chunk(g + 1, off, w, carry[1])
            acc, macc = sweep(both, (acc0, macc0))
        else:
            acc = sweep(functools.partial(pv_chunk, g, m), acc0)
        o = acc[:, :ATTN_HEAD_DIM] / acc[:, ATTN_HEAD_DIM:]
        for r in range(KV_GROUP):
            h = g * KV_GROUP + r
            hs = slice(h * ATTN_HEAD_DIM, (h + 1) * ATTN_HEAD_DIM)
            o_ref[:, hs] = (o[r * tq:(r + 1) * tq, :] * _silu(az_ref[:, hs])).astype(o_ref.dtype)


def _attn(proj, kn, vb, iklo, ikhi, tab_a, tab_i, qnw, lead, k_sel, bsz, tp):
    tq = ROW_BLOCK
    nq = tp // tq
    sb = _pick_tile(nq, 3, 1)
    tk = sb * tq
    tkw = ATTN_WIDE_CHUNKS * tk
    ns = nq // sb

    def seg(name):
        w = _DST[name][1]
        cbk = _col_block(name)
        return pl.BlockSpec((tq, w), lambda b, i: (b * nq + i, cbk))

    def seg_sb(name):
        w = _DST[name][1]
        cbk = _col_block(name)
        return pl.BlockSpec((tk, w), lambda b, i: (b * ns + i // sb, cbk), pipeline_mode=pl.Buffered(1))

    tab_sb = pl.BlockSpec((tk, 3 * LANES), lambda b, i: (i // sb, 0), pipeline_mode=pl.Buffered(1))
    vec = pl.BlockSpec((1, LANES), lambda b, i: (0, 0))
    keys = lambda w: pl.BlockSpec((tp, w), lambda b, i: (b, 0), pipeline_mode=pl.Buffered(1))
    return pl.pallas_call(
        functools.partial(_attn_kernel, lead, k_sel, tk, tkw),
        grid=(bsz, nq),
        in_specs=[seg_sb("q"), seg_sb("iq"), seg_sb("iw"), seg("az"), tab_sb, tab_sb, vec,
                  keys(KV_WIDTH), keys(KV_WIDTH), keys(LANES), keys(LANES)],
        out_specs=pl.BlockSpec((tq, ATTN_WIDTH), lambda b, i: (b * nq + i, 0)),
        out_shape=jax.ShapeDtypeStruct((bsz * tp, ATTN_WIDTH), MM),
        scratch_shapes=[pltpu.VMEM((tk, tp), jnp.int32),
                        pltpu.VMEM((IDX_HEADS, tk, LANES), jnp.float32),
                        pltpu.VMEM((2, KV_GROUP * tq, tp), jnp.float32),
                        pltpu.VMEM((sb, ATTN_HEADS * tq, ATTN_HEAD_DIM), MM)],
        compiler_params=_params(("parallel", "arbitrary")),
        name="attn",
    )(proj, proj, proj, proj, tab_a, tab_i, qnw, kn, vb, iklo, ikhi)


def _merge_kernel(l_ref, ys_ref, ya_ref, gs_ref, ga_ref, ws_ref, wa_ref, o_ref):
    del l_ref
    ys = jnp.dot(ys_ref[...], ws_ref[...], preferred_element_type=jnp.float32)
    ya = jnp.dot(ya_ref[...], wa_ref[...], preferred_element_type=jnp.float32)
    o_ref[...] = (_sigmoid(gs_ref[...]) * ys + _sigmoid(ga_ref[...]) * ya).astype(o_ref.dtype)


def _merge(l_arr, ys, ya, proj, ws, wa):
    m = ys.shape[0]
    tm = _pick_tile(m, 1056, 8)
    tn = 512
    gsb = _DST["gs"][0] // tn
    gab = _DST["ga"][0] // tn
    return pl.pallas_call(
        _merge_kernel,
        grid_spec=pltpu.PrefetchScalarGridSpec(
            num_scalar_prefetch=1, grid=(m // tm, D_MODEL // tn),
            in_specs=[pl.BlockSpec((tm, D_INNER), lambda i, j, l: (i, 0)),
                      pl.BlockSpec((tm, ATTN_WIDTH), lambda i, j, l: (i, 0)),
                      pl.BlockSpec((tm, tn), lambda i, j, l: (i, gsb + j)),
                      pl.BlockSpec((tm, tn), lambda i, j, l: (i, gab + j)),
                      pl.BlockSpec((None, D_INNER, tn), lambda i, j, l: (l[0], 0, j)),
                      pl.BlockSpec((None, ATTN_WIDTH, tn), lambda i, j, l: (l[0], 0, j))],
            out_specs=pl.BlockSpec((tm, tn), lambda i, j, l: (i, j))),
        out_shape=jax.ShapeDtypeStruct((m, D_MODEL), MM),
        compiler_params=_params(("parallel", "arbitrary")),
        name="merge",
    )(l_arr, ys, ya, proj, proj, ws, wa)


def _outproj_kernel(lead, tp, l_ref, mg_ref, h_ref, w_ref, o_ref):
    del l_ref
    tm = mg_ref.shape[0]
    upd = jnp.dot(mg_ref[...], w_ref[...], preferred_element_type=jnp.float32)
    rows = pl.program_id(0) * tm + lax.broadcasted_iota(jnp.int32, (tm, 1), 0)
    o_ref[...] = jnp.where(rows % tp >= lead, h_ref[...] + upd, 0.0)


def _outproj(l_arr, mg, h2, wo, lead, tp):
    m = mg.shape[0]
    tm = _pick_tile(m, 1056, 8)
    tn = 512
    return pl.pallas_call(
        functools.partial(_outproj_kernel, lead, tp),
        grid_spec=pltpu.PrefetchScalarGridSpec(
            num_scalar_prefetch=1, grid=(m // tm, D_MODEL // tn),
            in_specs=[pl.BlockSpec((tm, D_MODEL), lambda i, j, l: (i, 0)),
                      pl.BlockSpec((tm, tn), lambda i, j, l: (i, j)),
                      pl.BlockSpec((None, D_MODEL, tn), lambda i, j, l: (l[0], 0, j))],
            out_specs=pl.BlockSpec((tm, tn), lambda i, j, l: (i, j))),
        out_shape=jax.ShapeDtypeStruct((m, D_MODEL), jnp.float32),
        input_output_aliases={2: 0},
        compiler_params=_params(("parallel", "arbitrary")),
        name="outproj",
    )(l_arr, mg, h2, wo)


def _rope_table(pos, head_dim):
    rot = head_dim // ROPE_FRACTION
    half = rot // 2
    inv = ROPE_THETA ** (-jnp.arange(0, rot, 2, dtype=jnp.float32) / rot)
    ang = pos.astype(jnp.float32)[:, None] * inv[None, :]
    cos, sin = jnp.cos(ang), jnp.sin(ang)
    n = pos.shape[0]
    rest = head_dim - rot
    c = jnp.concatenate([cos, cos, jnp.ones((n, rest), jnp.float32)], axis=1)
    s_a = jnp.concatenate([-sin, jnp.zeros((n, head_dim - half), jnp.float32)], axis=1)
    s_b = jnp.concatenate([jnp.zeros((n, half), jnp.float32), sin, jnp.zeros((n, rest), jnp.float32)], axis=1)
    rep = LANES // head_dim
    return jnp.concatenate([jnp.tile(c, (1, rep)), jnp.tile(s_a, (1, rep)), jnp.tile(s_b, (1, rep))], axis=1)


def _pad_lanes(a, width=LANES):
    return jnp.pad(a, [(0, 0)] * (a.ndim - 1) + [(0, width - a.shape[-1])])


def _prep_w_in(w_in):
    depth, d, _ = w_in.shape
    parts = []
    for name, (_, w) in _DST.items():
        s0, sw = _SRC[name]
        parts.append(_pad_lanes(w_in[:, :, s0:s0 + sw], w))
    used = sum(w for _, w in _DST.values())
    parts.append(jnp.zeros((depth, d, NP - used), w_in.dtype))
    return jnp.concatenate(parts, axis=-1).astype(MM)


def kernel(x, meta_tokens, norm_w, w_in, conv_w, conv_b, dt_bias, a_log, d_skip, ssm_norm_w,
           w_ssm_out, q_norm_w, k_norm_w, idx_k_norm_w, w_attn_out, w_out):
    bsz, seq, d = x.shape
    depth = norm_w.shape[0]
    assert d == D_MODEL and w_in.shape[-1] == N_IN
    t = seq + N_META
    lead = (-t) % ROW_BLOCK
    tp = t + lead
    k_sel = min(TOPK_MAX, seq // 4)

    meta = jnp.broadcast_to(meta_tokens[None].astype(x.dtype), (bsz, N_META, d))
    h = jnp.concatenate([jnp.zeros((bsz, lead, d), x.dtype), meta, x], axis=1).reshape(bsz * tp, d)

    pos = jnp.arange(tp) - lead
    tab_a = _rope_table(pos, ATTN_HEAD_DIM)
    tab_i = _rope_table(pos, IDX_DIM)

    wp = _prep_w_in(w_in)
    ws = w_ssm_out.astype(MM)
    wa = w_attn_out.astype(MM)
    wo = w_out.astype(MM)
    cw_x, cw_b, cw_c = (conv_w[:, :, :D_INNER], conv_w[:, :, D_INNER:D_INNER + BC_WIDTH],
                        conv_w[:, :, D_INNER + BC_WIDTH:])
    cb_x, cb_b, cb_c = (conv_b[:, None, :D_INNER], conv_b[:, None, D_INNER:D_INNER + BC_WIDTH],
                        conv_b[:, None, D_INNER + BC_WIDTH:])
    dtb = _pad_lanes(dt_bias)[:, None, :]
    alog = _pad_lanes(a_log)[:, None, :]
    dsk = jnp.repeat(d_skip, SSM_HEAD_DIM, axis=-1)[:, None, :]
    snw = ssm_norm_w[:, None, :]
    iknw = _pad_lanes(idx_k_norm_w)[:, None, :]
    qnw = q_norm_w[:, None, :]
    knw = k_norm_w[:, None, :]
    nw = norm_w[:, None, :]

    def layer(l, h):
        l_arr = jnp.reshape(l, (1,)).astype(jnp.int32)
        proj = _inproj(l_arr, h, nw[l], wp)
        ys = _ssd(proj, lead, bsz, tp, cw_x[l], cw_b[l], cw_c[l], cb_x[l], cb_b[l], cb_c[l],
                  dtb[l], alog[l], dsk[l], snw[l])
        kn, vb, iklo, ikhi = _kprep(proj, tab_a, tab_i, knw[l], iknw[l], tp)
        ya = _attn(proj, kn, vb, iklo, ikhi, tab_a, tab_i, qnw[l], lead, k_sel, bsz, tp)
        mg = _merge(l_arr, ys, ya, proj, ws, wa)
        return _outproj(l_arr, mg, h, wo, lead, tp)

    h = lax.fori_loop(0, depth, layer, h)
    return h.reshape(bsz, tp, d)[:, lead + N_META:]
```

```python
import functools

import jax
import jax.numpy as jnp
from jax import lax
from jax.experimental import pallas as pl
from jax.experimental.pallas import tpu as pltpu

D_MODEL = 2048
N_META = 16
D_INNER = 2 * D_MODEL
SSM_HEAD_DIM = 64
SSM_HEADS = D_INNER // SSM_HEAD_DIM
SSM_GROUPS = 8
SSM_STATE = 128
CONV_WIDTH = 4
assert CONV_WIDTH == 4
BC_WIDTH = SSM_GROUPS * SSM_STATE
GROUP_WIDTH = D_INNER // SSM_GROUPS
ATTN_HEADS = 16
ATTN_KV_HEADS = 4
ATTN_HEAD_DIM = 128
ATTN_WIDTH = ATTN_HEADS * ATTN_HEAD_DIM
KV_WIDTH = ATTN_KV_HEADS * ATTN_HEAD_DIM
KV_GROUP = ATTN_HEADS // ATTN_KV_HEADS
IDX_HEADS = 16
IDX_DIM = 64
IDX_WIDTH = IDX_HEADS * IDX_DIM
TOPK_MAX = 256
ROPE_THETA = 500000.0
ROPE_FRACTION = 4
EPS = 1e-6

LANES = 128
ROW_BLOCK = 128
ATTN_WIDE_CHUNKS = 4
VMEM_LIMIT = 56 * 1024 * 1024

MM = jnp.bfloat16
NEG_BIG = -1e30
LOG2E = 1.4426950408889634
INT_MIN = -2 ** 31

_SRC = {}
_o = 0
for _n, _w in (("z", D_INNER), ("x", D_INNER), ("b", BC_WIDTH), ("c", BC_WIDTH), ("dt", SSM_HEADS),
               ("q", ATTN_WIDTH), ("k", KV_WIDTH), ("v", KV_WIDTH), ("az", ATTN_WIDTH),
               ("iq", IDX_WIDTH), ("ik", IDX_DIM), ("iw", IDX_HEADS), ("gs", D_MODEL), ("ga", D_MODEL)):
    _SRC[_n] = (_o, _w)
    _o += _w
N_IN = _o

_DST = {}
_o = 0
for _n, _w in (("z", 4096), ("x", 4096), ("q", 2048), ("az", 2048), ("gs", 2048), ("ga", 2048),
               ("b", 1024), ("c", 1024), ("iq", 1024), ("k", 512), ("v", 512),
               ("dt", 128), ("ik", 128), ("iw", 128)):
    assert _o % _w == 0
    _DST[_n] = (_o, _w)
    _o += _w
PROJ_TILE_N = 512
NP = -(-_o // PROJ_TILE_N) * PROJ_TILE_N


def _col_block(name):
    off, w = _DST[name]
    return off // w


def _pick_tile(n, target, mult):
    best = None
    for t in range(mult, min(n, target) + 1, mult):
        if n % t == 0:
            best = t
    assert best is not None, (n, target, mult)
    return best


def _sigmoid(v):
    return 0.5 + 0.5 * jnp.tanh(0.5 * v)


def _silu(v):
    h = 0.5 * v
    return h + h * jnp.tanh(h)


def _dot(a, b):
    return jnp.dot(a.astype(MM), b.astype(MM), preferred_element_type=jnp.float32)


def _dot_nt(a, b):
    return lax.dot_general(a.astype(MM), b.astype(MM), (((1,), (1,)), ((), ())),
                           preferred_element_type=jnp.float32)


def _dot_tn(a, b):
    return lax.dot_general(a.astype(MM), b.astype(MM), (((0,), (0,)), ((), ())),
                           preferred_element_type=jnp.float32)


def _params(sem):
    return pltpu.CompilerParams(dimension_semantics=sem, vmem_limit_bytes=VMEM_LIMIT)


def _inproj_kernel(l_ref, h_ref, nw_ref, w_ref, o_ref, hn_ref):
    del l_ref

    @pl.when(pl.program_id(1) == 0)
    def _():
        x = h_ref[...]
        ms = jnp.mean(x * x, axis=-1, keepdims=True)
        hn_ref[...] = (x * lax.rsqrt(ms + EPS) * nw_ref[...]).astype(hn_ref.dtype)

    o_ref[...] = jnp.dot(hn_ref[...], w_ref[...], preferred_element_type=jnp.float32)


def _inproj(l_arr, h2, nw, wp):
    m, d = h2.shape
    tm = _pick_tile(m, 1056, 8)
    tn = PROJ_TILE_N
    return pl.pallas_call(
        _inproj_kernel,
        grid_spec=pltpu.PrefetchScalarGridSpec(
            num_scalar_prefetch=1, grid=(m // tm, NP // tn),
            in_specs=[pl.BlockSpec((tm, d), lambda i, j, l: (i, 0)),
                      pl.BlockSpec((1, d), lambda i, j, l: (0, 0)),
                      pl.BlockSpec((None, d, tn), lambda i, j, l: (l[0], 0, j))],
            out_specs=pl.BlockSpec((tm, tn), lambda i, j, l: (i, j)),
            scratch_shapes=[pltpu.VMEM((tm, d), MM)]),
        out_shape=jax.ShapeDtypeStruct((m, NP), jnp.float32),
        compiler_params=_params(("parallel", "arbitrary")),
        name="inproj",
    )(l_arr, h2, nw, wp)


def _split3(a):
    hi = a.astype(jnp.bfloat16)
    r1 = a - hi.astype(jnp.float32)
    mid = r1.astype(jnp.bfloat16)
    lo = (r1 - mid.astype(jnp.float32)).astype(jnp.bfloat16)
    return hi, mid, lo


def _expand_heads(a, e):
    hi, mid, lo = _split3(a)
    f = functools.partial(jnp.dot, preferred_element_type=jnp.float32)
    return f(hi, e) + f(mid, e) + f(lo, e)


def _ssd_kernel(lead, z_ref, x_ref, b_ref, c_ref, dt_ref, cwx_ref, cwb_ref, cwc_ref,
                cbx_ref, cbb_ref, cbc_ref, dtb_ref, alog_ref, dsk_ref, nw_ref,
                o_ref, extx, extb, extc, extpx, extpb, extpc, state, ybuf, e_ref):
    f32 = jnp.float32
    ci = pl.program_id(1)
    q = ROW_BLOCK

    @pl.when(ci == 0)
    def _():
        for ref in (extx, extb, extc, extpx, extpb, extpc):
            ref[0:8, :] = jnp.zeros((8, ref.shape[1]), f32)
        state[...] = jnp.zeros(state.shape, f32)
        hrow = lax.broadcasted_iota(jnp.int32, (LANES, D_INNER), 0)
        hcol = lax.broadcasted_iota(jnp.int32, (LANES, D_INNER), 1) // SSM_HEAD_DIM
        e_ref[...] = (hrow == hcol).astype(jnp.bfloat16)

    def conv_silu(u_ref, ext, extp, w_ref, bias_ref):
        ext[8:8 + q, :] = u_ref[...]
        u0 = ext[8:8 + q, :]
        u1 = ext[7:7 + q, :]
        extp[8:8 + q, :] = w_ref[1:2, :] * u0 + w_ref[0:1, :] * u1
        acc = bias_ref[...] + w_ref[3:4, :] * u0 + w_ref[2:3, :] * u1 + extp[6:6 + q, :]
        ext[0:8, :] = ext[q:q + 8, :]
        extp[0:8, :] = extp[q:q + 8, :]
        return _silu(acc)

    rows = ci * q + lax.broadcasted_iota(jnp.int32, (q, 1), 0)
    valid = rows >= lead

    xa = jnp.where(valid, conv_silu(x_ref, extx, extpx, cwx_ref, cbx_ref), 0.0)
    ba = conv_silu(b_ref, extb, extpb, cwb_ref, cbb_ref)
    ca = conv_silu(c_ref, extc, extpc, cwc_ref, cbc_ref)

    dtr = dt_ref[...] + dtb_ref[...]
    dt = jnp.maximum(dtr, 0.0) + jnp.log(1.0 + jnp.exp(-jnp.abs(dtr)))
    dt = jnp.where(valid, dt, 0.0)
    a = -jnp.exp(alog_ref[...])
    ri = lax.broadcasted_iota(jnp.int32, (q, q), 0)
    cj = lax.broadcasted_iota(jnp.int32, (q, q), 1)
    causal = ri >= cj
    tri = causal.astype(f32)
    acs = jnp.dot(tri, dt * a, preferred_element_type=f32, precision=lax.Precision.HIGHEST)
    acs_t = acs.T
    eacs = jnp.exp(acs)
    dte = jnp.exp(acs[q - 1:q, :] - acs)

    e = e_ref[...]
    dt_x = _expand_heads(dt, e)
    eacs_x = _expand_heads(eacs, e)
    dte_x = _expand_heads(dte, e)

    xdt = xa * dt_x
    xd_end = (xdt * dte_x).astype(MM)
    xdt_m = xdt.astype(MM)
    lane = lax.broadcasted_iota(jnp.int32, (q, LANES), 1)
    lo_half = lane < SSM_HEAD_DIM

    heads_per_group = SSM_HEADS // SSM_GROUPS
    for g in range(SSM_GROUPS):
        gs = slice(g * GROUP_WIDTH, (g + 1) * GROUP_WIDTH)
        bg = ba[:, g * SSM_STATE:(g + 1) * SSM_STATE]
        cg = ca[:, g * SSM_STATE:(g + 1) * SSM_STATE]
        cb = _dot_nt(cg, bg)
        h_in = state[g]
        y_off = _dot(cg, h_in) * eacs_x[:, gs]
        s_new = _dot_tn(bg, xd_end[:, gs])
        state[g] = h_in * eacs_x[q - 1:q, gs] + s_new
        for pr in range(heads_per_group // 2):
            h0 = g * heads_per_group + 2 * pr
            ms = []
            for hh in (h0, h0 + 1):
                diff = acs[:, hh:hh + 1] - acs_t[hh:hh + 1, :]
                lmat = jnp.exp(jnp.where(causal, diff, -jnp.inf))
                ms.append((cb * lmat).astype(MM))
            lhs = jnp.concatenate(ms, axis=1)
            xp = xdt_m[:, h0 * SSM_HEAD_DIM:(h0 + 2) * SSM_HEAD_DIM]
            zero = jnp.zeros_like(xp)
            rhs = jnp.concatenate([jnp.where(lo_half, xp, zero), jnp.where(lo_half, zero, xp)], axis=0)
            yd = jnp.dot(lhs, rhs, preferred_element_type=f32)
            cs = slice(h0 * SSM_HEAD_DIM, (h0 + 2) * SSM_HEAD_DIM)
            ybuf[:, cs] = yd + y_off[:, 2 * pr * SSM_HEAD_DIM:(2 * pr + 2) * SSM_HEAD_DIM]

    y = ybuf[...] + dsk_ref[...] * xa
    gz = y * _silu(z_ref[...])
    for g in range(SSM_GROUPS):
        gs = slice(g * GROUP_WIDTH, (g + 1) * GROUP_WIDTH)
        gg = gz[:, gs]
        ms = jnp.mean(gg * gg, axis=-1, keepdims=True)
        o_ref[:, gs] = (gg * lax.rsqrt(ms + EPS) * nw_ref[:, gs]).astype(o_ref.dtype)


def _ssd(proj, lead, bsz, tp, cwx, cwb, cwc, cbx, cbb, cbc, dtb, alog, dsk, nw):
    nc = tp // ROW_BLOCK
    q = ROW_BLOCK

    def seg(name):
        w = _DST[name][1]
        cbk = _col_block(name)
        return pl.BlockSpec((q, w), lambda b, c: (b * nc + c, cbk))

    def full(a):
        return pl.BlockSpec(a.shape, lambda b, c: (0, 0))

    small = (cwx, cwb, cwc, cbx, cbb, cbc, dtb, alog, dsk, nw)
    return pl.pallas_call(
        functools.partial(_ssd_kernel, lead),
        grid=(bsz, nc),
        in_specs=[seg("z"), seg("x"), seg("b"), seg("c"), seg("dt")] + [full(a) for a in small],
        out_specs=pl.BlockSpec((q, D_INNER), lambda b, c: (b * nc + c, 0)),
        out_shape=jax.ShapeDtypeStruct((bsz * tp, D_INNER), MM),
        scratch_shapes=[pltpu.VMEM((q + 8, D_INNER), jnp.float32),
                        pltpu.VMEM((q + 8, BC_WIDTH), jnp.float32),
                        pltpu.VMEM((q + 8, BC_WIDTH), jnp.float32),
                        pltpu.VMEM((q + 8, D_INNER), jnp.float32),
                        pltpu.VMEM((q + 8, BC_WIDTH), jnp.float32),
                        pltpu.VMEM((q + 8, BC_WIDTH), jnp.float32),
                        pltpu.VMEM((SSM_GROUPS, SSM_STATE, GROUP_WIDTH), jnp.float32),
                        pltpu.VMEM((q, D_INNER), jnp.float32),
                        pltpu.VMEM((LANES, D_INNER), jnp.bfloat16)],
        compiler_params=_params(("parallel", "arbitrary")),
        name="ssd",
    )(proj, proj, proj, proj, proj, *small)


def _split2(a):
    hi = a.astype(jnp.bfloat16)
    lo = (a - hi.astype(jnp.float32)).astype(jnp.bfloat16)
    return hi, lo


def _dot2(a, b):
    hi, lo = _split2(a)
    f = functools.partial(jnp.dot, preferred_element_type=jnp.float32)
    return f(hi, b) + f(lo, b)


def _rot_half_matrix(head_dim):
    half = head_dim // ROPE_FRACTION // 2
    rj = lax.broadcasted_iota(jnp.int32, (LANES, LANES), 0)
    ci = lax.broadcasted_iota(jnp.int32, (LANES, LANES), 1)
    cm = ci % head_dim
    neg = (cm < half) & (rj == ci + half)
    pos = (cm >= half) & (cm < 2 * half) & (rj == ci - half)
    return jnp.where(neg, -1.0, jnp.where(pos, 1.0, 0.0)).astype(jnp.bfloat16)


def _rope128(x, tab, half):
    cos = tab[:, 0:LANES]
    s_a = tab[:, LANES:2 * LANES]
    s_b = tab[:, 2 * LANES:3 * LANES]
    return x * cos + pltpu.roll(x, LANES - half, 1) * s_a + pltpu.roll(x, half, 1) * s_b


def _rms128(x, w, width):
    ms = jnp.sum(x * x, axis=-1, keepdims=True) * (1.0 / width)
    return x * lax.rsqrt(ms + EPS) * w


def _kprep_kernel(k_ref, v_ref, ik_ref, ta_ref, ti_ref, knw_ref, iknw_ref,
                  ko_ref, vo_ref, iklo_ref, ikhi_ref):
    ta = ta_ref[...]
    for h in range(ATTN_KV_HEADS):
        cs = slice(h * ATTN_HEAD_DIM, (h + 1) * ATTN_HEAD_DIM)
        kn = _rms128(k_ref[:, cs], knw_ref[...], ATTN_HEAD_DIM)
        ko_ref[:, cs] = _rope128(kn, ta, ATTN_HEAD_DIM // ROPE_FRACTION // 2).astype(ko_ref.dtype)
    vo_ref[...] = v_ref[...].astype(vo_ref.dtype)
    ikn = _rms128(ik_ref[...], iknw_ref[...], IDX_DIM)
    ikn = _rope128(ikn, ti_ref[...], IDX_DIM // ROPE_FRACTION // 2)
    iklo_ref[...] = ikn.astype(iklo_ref.dtype)
    ikhi_ref[...] = pltpu.roll(ikn, IDX_DIM, 1).astype(ikhi_ref.dtype)


def _kprep(proj, tab_a, tab_i, knw, iknw, tp):
    m = proj.shape[0]
    tr = _pick_tile(tp, 1408, 8)
    nt = tp // tr

    def seg(name):
        w = _DST[name][1]
        cbk = _col_block(name)
        return pl.BlockSpec((tr, w), lambda i: (i, cbk))

    tab = pl.BlockSpec((tr, 3 * LANES), lambda i: (i % nt, 0))
    vec = pl.BlockSpec((1, LANES), lambda i: (0, 0))
    row = lambda w: pl.BlockSpec((tr, w), lambda i: (i, 0))
    return pl.pallas_call(
        _kprep_kernel,
        grid=(m // tr,),
        in_specs=[seg("k"), seg("v"), seg("ik"), tab, tab, vec, vec],
        out_specs=[row(KV_WIDTH), row(KV_WIDTH), row(LANES), row(LANES)],
        out_shape=[jax.ShapeDtypeStruct((m, KV_WIDTH), MM), jax.ShapeDtypeStruct((m, KV_WIDTH), MM),
                   jax.ShapeDtypeStruct((m, LANES), MM), jax.ShapeDtypeStruct((m, LANES), MM)],
        compiler_params=_params(("parallel",)),
        name="kprep",
    )(proj, proj, proj, tab_a, tab_i, knw, iknw)


def _attn_kernel(lead, k_sel, tk, tkw, q_ref, iq_ref, iw_ref, az_ref, ta_ref, ti_ref, qnw_ref,
                 kn_ref, vb_ref, iklo_ref, ikhi_ref, o_ref, key_ref, iwb_ref, s_ref, qs_ref):
    f32 = jnp.float32
    tq = ROW_BLOCK
    sb = tk // tq
    ts = sb * tq
    nlb = tk // LANES
    qi = pl.program_id(1)
    si = qi // sb

    @pl.when(qi % sb == 0)
    def _():
        nch = si + 1
        ti = ti_ref[...]
        npair = IDX_HEADS // 2
        xi = jnp.concatenate([iq_ref[:, p * LANES:(p + 1) * LANES] for p in range(npair)], axis=0)
        roti = _dot2(xi, _rot_half_matrix(IDX_DIM))
        cos_i, sin_i = ti[:, 0:LANES], ti[:, 2 * LANES:3 * LANES] - ti[:, LANES:2 * LANES]
        iq_all = xi.reshape(npair, ts, LANES) * cos_i[None] + roti.reshape(npair, ts, LANES) * sin_i[None]
        iq_all = iq_all.astype(MM).reshape(npair * ts, LANES)
        iw = iw_ref[...] * (IDX_HEADS ** -0.5 * IDX_DIM ** -0.5)
        for h in range(IDX_HEADS):
            iwb_ref[h] = jnp.broadcast_to(iw[:, h:h + 1], (ts, LANES))

        qrow = si * ts + lax.broadcasted_iota(jnp.int32, (ts, 1), 0)
        kcol0 = lax.broadcasted_iota(jnp.int32, (1, tk), 1)

        def score_chunk(j, carry):
            off = pl.multiple_of(j * tk, tk)
            ik2 = jnp.concatenate([iklo_ref[pl.ds(off, tk), :], ikhi_ref[pl.ds(off, tk), :]], axis=0)
            cols = [None] * nlb
            for half in range(2):
                p0 = half * (npair // 2)
                l_both = _dot_nt(iq_all[p0 * ts:(p0 + npair // 2) * ts], ik2)
                for c in range(nlb):
                    le = slice(c * LANES, (c + 1) * LANES)
                    lo = slice(tk + c * LANES, tk + (c + 1) * LANES)
                    for pp in range(npair // 2):
                        rs = slice(pp * ts, (pp + 1) * ts)
                        p = p0 + pp
                        term = (iwb_ref[2 * p] * jnp.maximum(l_both[rs, le], 0.0)
                                + iwb_ref[2 * p + 1] * jnp.maximum(l_both[rs, lo], 0.0))
                        cols[c] = term if cols[c] is None else cols[c] + term
            sc = jnp.concatenate(cols, axis=1)
            bits = pltpu.bitcast(sc, jnp.int32)
            keys = bits ^ ((bits >> 31) & jnp.int32(0x7FFFFFFF))
            kcol = off + kcol0
            visible = (kcol <= qrow) & (kcol >= lead)
            key_ref[:, pl.ds(off, tk)] = jnp.where(visible, keys, jnp.int32(INT_MIN))
            return carry

        lax.fori_loop(0, nch, score_chunk, 0)

        ones_cnt = jnp.ones((LANES, LANES), jnp.bfloat16)

        def bit_step(bi, t):
            cand_s = t ^ lax.shift_left(jnp.int32(1), 31 - bi)
            accs = []
            for r in range(sb):
                rs = slice(r * tq, (r + 1) * tq)
                cand_r = cand_s[rs]

                def count_blocks(off, nblk, acc, rs=rs, cand_r=cand_r):
                    for c in range(nblk):
                        kk = key_ref[rs, pl.ds(pl.multiple_of(off + c * LANES, LANES), LANES)]
                        acc = acc + jnp.where(kk >= cand_r, 1.0, 0.0)
                    return acc

                acc = lax.fori_loop(0, nch - 1,
                                    lambda j, a, f=count_blocks: f(pl.multiple_of(j * tk, tk), nlb, a),
                                    jnp.zeros((tq, LANES), f32))
                accs.append(count_blocks(pl.multiple_of((nch - 1) * tk, tk), (r + 1) * (tq // LANES), acc))
            acc = jnp.concatenate(accs, axis=0).astype(jnp.bfloat16)
            cnt = jnp.dot(acc, ones_cnt, preferred_element_type=f32)
            return jnp.where(cnt >= float(k_sel), cand_s, t)

        t_s = lax.fori_loop(0, 32, bit_step, jnp.full((ts, LANES), INT_MIN, jnp.int32))
        thr = jnp.maximum(t_s, jnp.int32(INT_MIN + 1))

        def bias_chunk(j, carry):
            off = pl.multiple_of(j * tk, tk)
            for c in range(nlb):
                cols = pl.ds(pl.multiple_of(off + c * LANES, LANES), LANES)
                kk = key_ref[:, cols]
                key_ref[:, cols] = pltpu.bitcast(jnp.where(kk >= thr, 0.0, NEG_BIG), jnp.int32)
            return carry

        lax.fori_loop(0, nch, bias_chunk, 0)

        ta = ta_ref[...]
        cos_a, sin_a = ta[:, 0:LANES], ta[:, 2 * LANES:3 * LANES] - ta[:, LANES:2 * LANES]
        rot_m = _rot_half_matrix(ATTN_HEAD_DIM)
        scale = ATTN_HEAD_DIM ** -0.5 * LOG2E
        for g in range(ATTN_KV_HEADS):
            heads = range(g * KV_GROUP, (g + 1) * KV_GROUP)
            xs = jnp.concatenate([q_ref[:, h * ATTN_HEAD_DIM:(h + 1) * ATTN_HEAD_DIM] for h in heads], axis=0)
            ms = _dot2(xs * xs, ones_cnt) * (1.0 / ATTN_HEAD_DIM)
            xn = xs * lax.rsqrt(ms + EPS) * qnw_ref[...]
            rot = _dot2(xn, rot_m)
            qr = xn.reshape(KV_GROUP, ts, LANES) * cos_a[None] + rot.reshape(KV_GROUP, ts, LANES) * sin_a[None]
            qr = (qr * scale).astype(MM)
            for hh in range(KV_GROUP):
                for r in range(sb):
                    row0 = (g * KV_GROUP + hh) * tq
                    qs_ref[r, row0:row0 + tq, :] = qr[hh, r * tq:(r + 1) * tq, :]

    rows4 = KV_GROUP * tq
    brow = pl.multiple_of((qi % sb) * tq, tq)
    macc0 = jnp.full((rows4, LANES), NEG_BIG, f32)
    acc0 = jnp.zeros((rows4, 2 * ATTN_HEAD_DIM), f32)

    def logits_chunk(g, off, w, macc):
        cs = slice(g * ATTN_HEAD_DIM, (g + 1) * ATTN_HEAD_DIM)
        q4 = qs_ref[qi % sb, g * rows4:(g + 1) * rows4, :]
        s = _dot_nt(q4, kn_ref[pl.ds(off, w), cs])
        bias = pltpu.bitcast(key_ref[pl.ds(brow, tq), pl.ds(off, w)], f32)
        s = (s.reshape(KV_GROUP, tq, w) + bias[None]).reshape(rows4, w)
        s_ref[g % 2, :, pl.ds(off, w)] = s
        for c in range(w // LANES):
            macc = jnp.maximum(macc, s[:, c * LANES:(c + 1) * LANES])
        return macc

    def pv_chunk(g, m, off, w, acc):
        cs = slice(g * ATTN_HEAD_DIM, (g + 1) * ATTN_HEAD_DIM)
        p = jnp.exp2(s_ref[g % 2, :, pl.ds(off, w)] - m).astype(MM)
        v1 = jnp.concatenate([vb_ref[pl.ds(off, w), cs], jnp.ones((w, ATTN_HEAD_DIM), MM)], axis=1)
        return acc + jnp.dot(p, v1, preferred_element_type=f32)

    needed = qi * tq + tq
    tkm = tkw // 2
    nwide = needed // tkw
    nmid = (needed - nwide * tkw) // tkm
    base = nwide * tkw + nmid * tkm
    nnarrow = (needed - base + tk - 1) // tk

    def sweep(fn, carry):
        carry = lax.fori_loop(0, nwide, lambda j, c: fn(pl.multiple_of(j * tkw, tkw), tkw, c), carry)
        carry = lax.fori_loop(0, nmid, lambda j, c: fn(pl.multiple_of(nwide * tkw, tkm), tkm, c), carry)
        return lax.fori_loop(0, nnarrow, lambda j, c: fn(pl.multiple_of(base + j * tk, tk), tk, c), carry)

    macc = sweep(functools.partial(logits_chunk, 0), macc0)
    for g in range(ATTN_KV_HEADS):
        m = jnp.max(macc, axis=-1, keepdims=True)
        if g + 1 < ATTN_KV_HEADS:
            def both(off, w, carry, g=g, m=m):
                return pv_chunk(g, m, off, w, carry[0]), logits_chunk(g + 1, off, w, carry[1])
            acc, macc = sweep(both, (acc0, macc0))
        else:
            acc = sweep(functools.partial(pv_chunk, g, m), acc0)
        o = acc[:, :ATTN_HEAD_DIM] / acc[:, ATTN_HEAD_DIM:]
        for r in range(KV_GROUP):
            h = g * KV_GROUP + r
            hs = slice(h * ATTN_HEAD_DIM, (h + 1) * ATTN_HEAD_DIM)
            o_ref[:, hs] = (o[r * tq:(r + 1) * tq, :] * _silu(az_ref[:, hs])).astype(o_ref.dtype)


def _attn(proj, kn, vb, iklo, ikhi, tab_a, tab_i, qnw, lead, k_sel, bsz, tp):
    tq = ROW_BLOCK
    nq = tp // tq
    sb = _pick_tile(nq, 3, 1)
    tk = sb * tq
    tkw = ATTN_WIDE_CHUNKS * tk
    ns = nq // sb

    def seg(name):
        w = _DST[name][1]
        cbk = _col_block(name)
        return pl.BlockSpec((tq, w), lambda b, i: (b * nq + i, cbk))

    def seg_sb(name):
        w = _DST[name][1]
        cbk = _col_block(name)
        return pl.BlockSpec((tk, w), lambda b, i: (b * ns + i // sb, cbk), pipeline_mode=pl.Buffered(1))

    tab_sb = pl.BlockSpec((tk, 3 * LANES), lambda b, i: (i // sb, 0), pipeline_mode=pl.Buffered(1))
    vec = pl.BlockSpec((1, LANES), lambda b, i: (0, 0))
    keys = lambda w: pl.BlockSpec((tp, w), lambda b, i: (b, 0), pipeline_mode=pl.Buffered(1))
    return pl.pallas_call(
        functools.partial(_attn_kernel, lead, k_sel, tk, tkw),
        grid=(bsz, nq),
        in_specs=[seg_sb("q"), seg_sb("iq"), seg_sb("iw"), seg("az"), tab_sb, tab_sb, vec,
                  keys(KV_WIDTH), keys(KV_WIDTH), keys(LANES), keys(LANES)],
        out_specs=pl.BlockSpec((tq, ATTN_WIDTH), lambda b, i: (b * nq + i, 0)),
        out_shape=jax.ShapeDtypeStruct((bsz * tp, ATTN_WIDTH), MM),
        scratch_shapes=[pltpu.VMEM((tk, tp), jnp.int32),
                        pltpu.VMEM((IDX_HEADS, tk, LANES), jnp.float32),
                        pltpu.VMEM((2, KV_GROUP * tq, tp), jnp.float32),
                        pltpu.VMEM((sb, ATTN_HEADS * tq, ATTN_HEAD_DIM), MM)],
        compiler_params=_params(("parallel", "arbitrary")),
        name="attn",
    )(proj, proj, proj, proj, tab_a, tab_i, qnw, kn, vb, iklo, ikhi)


def _merge_kernel(l_ref, ys_ref, ya_ref, gs_ref, ga_ref, ws_ref, wa_ref, o_ref):
    del l_ref
    ys = jnp.dot(ys_ref[...], ws_ref[...], preferred_element_type=jnp.float32)
    ya = jnp.dot(ya_ref[...], wa_ref[...], preferred_element_type=jnp.float32)
    o_ref[...] = (_sigmoid(gs_ref[...]) * ys + _sigmoid(ga_ref[...]) * ya).astype(o_ref.dtype)


def _merge(l_arr, ys, ya, proj, ws, wa):
    m = ys.shape[0]
    tm = _pick_tile(m, 1056, 8)
    tn = 512
    gsb = _DST["gs"][0] // tn
    gab = _DST["ga"][0] // tn
    return pl.pallas_call(
        _merge_kernel,
        grid_spec=pltpu.PrefetchScalarGridSpec(
            num_scalar_prefetch=1, grid=(m // tm, D_MODEL // tn),
            in_specs=[pl.BlockSpec((tm, D_INNER), lambda i, j, l: (i, 0)),
                      pl.BlockSpec((tm, ATTN_WIDTH), lambda i, j, l: (i, 0)),
                      pl.BlockSpec((tm, tn), lambda i, j, l: (i, gsb + j)),
                      pl.BlockSpec((tm, tn), lambda i, j, l: (i, gab + j)),
                      pl.BlockSpec((None, D_INNER, tn), lambda i, j, l: (l[0], 0, j)),
                      pl.BlockSpec((None, ATTN_WIDTH, tn), lambda i, j, l: (l[0], 0, j))],
            out_specs=pl.BlockSpec((tm, tn), lambda i, j, l: (i, j))),
        out_shape=jax.ShapeDtypeStruct((m, D_MODEL), MM),
        compiler_params=_params(("parallel", "arbitrary")),
        name="merge",
    )(l_arr, ys, ya, proj, proj, ws, wa)


def _outproj_kernel(lead, tp, l_ref, mg_ref, h_ref, w_ref, o_ref):
    del l_ref
    tm = mg_ref.shape[0]
    upd = jnp.dot(mg_ref[...], w_ref[...], preferred_element_type=jnp.float32)
    rows = pl.program_id(0) * tm + lax.broadcasted_iota(jnp.int32, (tm, 1), 0)
    o_ref[...] = jnp.where(rows % tp >= lead, h_ref[...] + upd, 0.0)


def _outproj(l_arr, mg, h2, wo, lead, tp):
    m = mg.shape[0]
    tm = _pick_tile(m, 1056, 8)
    tn = 512
    return pl.pallas_call(
        functools.partial(_outproj_kernel, lead, tp),
        grid_spec=pltpu.PrefetchScalarGridSpec(
            num_scalar_prefetch=1, grid=(m // tm, D_MODEL // tn),
            in_specs=[pl.BlockSpec((tm, D_MODEL), lambda i, j, l: (i, 0)),
                      pl.BlockSpec((tm, tn), lambda i, j, l: (i, j)),
                      pl.BlockSpec((None, D_MODEL, tn), lambda i, j, l: (l[0], 0, j))],
            out_specs=pl.BlockSpec((tm, tn), lambda i, j, l: (i, j))),
        out_shape=jax.ShapeDtypeStruct((m, D_MODEL), jnp.float32),
        input_output_aliases={2: 0},
        compiler_params=_params(("parallel", "arbitrary")),
        name="outproj",
    )(l_arr, mg, h2, wo)


def _rope_table(pos, head_dim):
    rot = head_dim // ROPE_FRACTION
    half = rot // 2
    inv = ROPE_THETA ** (-jnp.arange(0, rot, 2, dtype=jnp.float32) / rot)
    ang = pos.astype(jnp.float32)[:, None] * inv[None, :]
    cos, sin = jnp.cos(ang), jnp.sin(ang)
    n = pos.shape[0]
    rest = head_dim - rot
    c = jnp.concatenate([cos, cos, jnp.ones((n, rest), jnp.float32)], axis=1)
    s_a = jnp.concatenate([-sin, jnp.zeros((n, head_dim - half), jnp.float32)], axis=1)
    s_b = jnp.concatenate([jnp.zeros((n, half), jnp.float32), sin, jnp.zeros((n, rest), jnp.float32)], axis=1)
    rep = LANES // head_dim
    return jnp.concatenate([jnp.tile(c, (1, rep)), jnp.tile(s_a, (1, rep)), jnp.tile(s_b, (1, rep))], axis=1)


def _pad_lanes(a, width=LANES):
    return jnp.pad(a, [(0, 0)] * (a.ndim - 1) + [(0, width - a.shape[-1])])


def _prep_w_in(w_in):
    depth, d, _ = w_in.shape
    parts = []
    run = None
    for name, (_, w) in _DST.items():
        s0, sw = _SRC[name]
        if sw == w and run is not None and run[1] == s0:
            run[1] = s0 + sw
            continue
        if run is not None:
            parts.append(w_in[:, :, run[0]:run[1]])
            run = None
        if sw == w:
            run = [s0, s0 + sw]
        else:
            parts.append(_pad_lanes(w_in[:, :, s0:s0 + sw], w))
    if run is not None:
        parts.append(w_in[:, :, run[0]:run[1]])
    used = sum(w for _, w in _DST.values())
    parts.append(jnp.zeros((depth, d, NP - used), w_in.dtype))
    return jnp.concatenate(parts, axis=-1).astype(MM)


def kernel(x, meta_tokens, norm_w, w_in, conv_w, conv_b, dt_bias, a_log, d_skip, ssm_norm_w,
           w_ssm_out, q_norm_w, k_norm_w, idx_k_norm_w, w_attn_out, w_out):
    bsz, seq, d = x.shape
    depth = norm_w.shape[0]
    assert d == D_MODEL and w_in.shape[-1] == N_IN
    t = seq + N_META
    lead = (-t) % ROW_BLOCK
    tp = t + lead
    k_sel = min(TOPK_MAX, seq // 4)

    meta = jnp.broadcast_to(meta_tokens[None].astype(x.dtype), (bsz, N_META, d))
    h = jnp.concatenate([jnp.zeros((bsz, lead, d), x.dtype), meta, x], axis=1).reshape(bsz * tp, d)

    pos = jnp.arange(tp) - lead
    tab_a = _rope_table(pos, ATTN_HEAD_DIM)
    tab_i = _rope_table(pos, IDX_DIM)

    wp = _prep_w_in(w_in)
    ws = w_ssm_out.astype(MM)
    wa = w_attn_out.astype(MM)
    wo = w_out.astype(MM)
    cw_x, cw_b, cw_c = (conv_w[:, :, :D_INNER], conv_w[:, :, D_INNER:D_INNER + BC_WIDTH],
                        conv_w[:, :, D_INNER + BC_WIDTH:])
    cb_x, cb_b, cb_c = (conv_b[:, None, :D_INNER], conv_b[:, None, D_INNER:D_INNER + BC_WIDTH],
                        conv_b[:, None, D_INNER + BC_WIDTH:])
    dtb = _pad_lanes(dt_bias)[:, None, :]
    alog = _pad_lanes(a_log)[:, None, :]
    dsk = jnp.repeat(d_skip, SSM_HEAD_DIM, axis=-1)[:, None, :]
    snw = ssm_norm_w[:, None, :]
    iknw = _pad_lanes(idx_k_norm_w)[:, None, :]
    qnw = q_norm_w[:, None, :]
    knw = k_norm_w[:, None, :]
    nw = norm_w[:, None, :]

    def layer(l, h):
        l_arr = jnp.reshape(l, (1,)).astype(jnp.int32)
        proj = _inproj(l_arr, h, nw[l], wp)
        ys = _ssd(proj, lead, bsz, tp, cw_x[l], cw_b[l], cw_c[l], cb_x[l], cb_b[l], cb_c[l],
                  dtb[l], alog[l], dsk[l], snw[l])
        kn, vb, iklo, ikhi = _kprep(proj, tab_a, tab_i, knw[l], iknw[l], tp)
        ya = _attn(proj, kn, vb, iklo, ikhi, tab_a, tab_i, qnw[l], lead, k_sel, bsz, tp)
        mg = _merge(l_arr, ys, ya, proj, ws, wa)
        return _outproj(l_arr, mg, h, wo, lead, tp)

    h = lax.fori_loop(0, depth, layer, h)
    return h.reshape(bsz, tp, d)[:, lead + N_META:]
```

```python
import functools

import jax
import jax.numpy as jnp
from jax import lax
from jax.experimental import pallas as pl
from jax.experimental.pallas import tpu as pltpu

D_MODEL = 2048
N_META = 16
D_INNER = 2 * D_MODEL
SSM_HEAD_DIM = 64
SSM_HEADS = D_INNER // SSM_HEAD_DIM
SSM_GROUPS = 8
SSM_STATE = 128
CONV_WIDTH = 4
assert CONV_WIDTH == 4
BC_WIDTH = SSM_GROUPS * SSM_STATE
GROUP_WIDTH = D_INNER // SSM_GROUPS
ATTN_HEADS = 16
ATTN_KV_HEADS = 4
ATTN_HEAD_DIM = 128
ATTN_WIDTH = ATTN_HEADS * ATTN_HEAD_DIM
KV_WIDTH = ATTN_KV_HEADS * ATTN_HEAD_DIM
KV_GROUP = ATTN_HEADS // ATTN_KV_HEADS
IDX_HEADS = 16
IDX_DIM = 64
IDX_WIDTH = IDX_HEADS * IDX_DIM
TOPK_MAX = 256
ROPE_THETA = 500000.0
ROPE_FRACTION = 4
EPS = 1e-6

LANES = 128
ROW_BLOCK = 128
ATTN_WIDE_CHUNKS = 4
VMEM_LIMIT = 56 * 1024 * 1024

MM = jnp.bfloat16
NEG_BIG = -1e30
LOG2E = 1.4426950408889634
INT_MIN = -2 ** 31

_SRC = {}
_o = 0
for _n, _w in (("z", D_INNER), ("x", D_INNER), ("b", BC_WIDTH), ("c", BC_WIDTH), ("dt", SSM_HEADS),
               ("q", ATTN_WIDTH), ("k", KV_WIDTH), ("v", KV_WIDTH), ("az", ATTN_WIDTH),
               ("iq", IDX_WIDTH), ("ik", IDX_DIM), ("iw", IDX_HEADS), ("gs", D_MODEL), ("ga", D_MODEL)):
    _SRC[_n] = (_o, _w)
    _o += _w
N_IN = _o

_DST = {}
_o = 0
for _n, _w in (("z", 4096), ("x", 4096), ("q", 2048), ("az", 2048), ("gs", 2048), ("ga", 2048),
               ("b", 1024), ("c", 1024), ("iq", 1024), ("k", 512), ("v", 512),
               ("dt", 128), ("ik", 128), ("iw", 128)):
    assert _o % _w == 0
    _DST[_n] = (_o, _w)
    _o += _w
PROJ_TILE_N = 512
NP = -(-_o // PROJ_TILE_N) * PROJ_TILE_N


def _col_block(name):
    off, w = _DST[name]
    return off // w


def _pick_tile(n, target, mult):
    best = None
    for t in range(mult, min(n, target) + 1, mult):
        if n % t == 0:
            best = t
    assert best is not None, (n, target, mult)
    return best


def _sigmoid(v):
    return 0.5 + 0.5 * jnp.tanh(0.5 * v)


def _silu(v):
    h = 0.5 * v
    return h + h * jnp.tanh(h)


def _dot(a, b):
    return jnp.dot(a.astype(MM), b.astype(MM), preferred_element_type=jnp.float32)


def _dot_nt(a, b):
    return lax.dot_general(a.astype(MM), b.astype(MM), (((1,), (1,)), ((), ())),
                           preferred_element_type=jnp.float32)


def _dot_tn(a, b):
    return lax.dot_general(a.astype(MM), b.astype(MM), (((0,), (0,)), ((), ())),
                           preferred_element_type=jnp.float32)


def _params(sem):
    return pltpu.CompilerParams(dimension_semantics=sem, vmem_limit_bytes=VMEM_LIMIT)


def _inproj_kernel(l_ref, h_ref, nw_ref, w_ref, o_ref, hn_ref):
    del l_ref

    @pl.when(pl.program_id(1) == 0)
    def _():
        x = h_ref[...]
        ms = jnp.mean(x * x, axis=-1, keepdims=True)
        hn_ref[...] = (x * lax.rsqrt(ms + EPS) * nw_ref[...]).astype(hn_ref.dtype)

    o_ref[...] = jnp.dot(hn_ref[...], w_ref[...], preferred_element_type=jnp.float32)


def _inproj(l_arr, h2, nw, wp):
    m, d = h2.shape
    tm = _pick_tile(m, 1056, 8)
    tn = PROJ_TILE_N
    return pl.pallas_call(
        _inproj_kernel,
        grid_spec=pltpu.PrefetchScalarGridSpec(
            num_scalar_prefetch=1, grid=(m // tm, NP // tn),
            in_specs=[pl.BlockSpec((tm, d), lambda i, j, l: (i, 0)),
                      pl.BlockSpec((1, d), lambda i, j, l: (0, 0)),
                      pl.BlockSpec((None, d, tn), lambda i, j, l: (l[0], 0, j))],
            out_specs=pl.BlockSpec((tm, tn), lambda i, j, l: (i, j)),
            scratch_shapes=[pltpu.VMEM((tm, d), MM)]),
        out_shape=jax.ShapeDtypeStruct((m, NP), jnp.float32),
        compiler_params=_params(("parallel", "arbitrary")),
        name="inproj",
    )(l_arr, h2, nw, wp)


def _split3(a):
    hi = a.astype(jnp.bfloat16)
    r1 = a - hi.astype(jnp.float32)
    mid = r1.astype(jnp.bfloat16)
    lo = (r1 - mid.astype(jnp.float32)).astype(jnp.bfloat16)
    return hi, mid, lo


def _expand_heads(a, e):
    hi, mid, lo = _split3(a)
    f = functools.partial(jnp.dot, preferred_element_type=jnp.float32)
    return f(hi, e) + f(mid, e) + f(lo, e)


def _ssd_kernel(lead, z_ref, x_ref, b_ref, c_ref, dt_ref, cwx_ref, cwb_ref, cwc_ref,
                cbx_ref, cbb_ref, cbc_ref, dtb_ref, alog_ref, dsk_ref, nw_ref,
                o_ref, extx, extb, extc, extpx, extpb, extpc, state, ybuf, e_ref):
    f32 = jnp.float32
    ci = pl.program_id(1)
    q = ROW_BLOCK

    @pl.when(ci == 0)
    def _():
        for ref in (extx, extb, extc, extpx, extpb, extpc):
            ref[0:8, :] = jnp.zeros((8, ref.shape[1]), f32)
        state[...] = jnp.zeros(state.shape, f32)
        hrow = lax.broadcasted_iota(jnp.int32, (LANES, D_INNER), 0)
        hcol = lax.broadcasted_iota(jnp.int32, (LANES, D_INNER), 1) // SSM_HEAD_DIM
        e_ref[...] = (hrow == hcol).astype(jnp.bfloat16)

    def conv_silu(u_ref, ext, extp, w_ref, bias_ref):
        ext[8:8 + q, :] = u_ref[...]
        u0 = ext[8:8 + q, :]
        u1 = ext[7:7 + q, :]
        extp[8:8 + q, :] = w_ref[1:2, :] * u0 + w_ref[0:1, :] * u1
        acc = bias_ref[...] + w_ref[3:4, :] * u0 + w_ref[2:3, :] * u1 + extp[6:6 + q, :]
        ext[0:8, :] = ext[q:q + 8, :]
        extp[0:8, :] = extp[q:q + 8, :]
        return _silu(acc)

    rows = ci * q + lax.broadcasted_iota(jnp.int32, (q, 1), 0)
    valid = rows >= lead

    xa = jnp.where(valid, conv_silu(x_ref, extx, extpx, cwx_ref, cbx_ref), 0.0)
    ba = conv_silu(b_ref, extb, extpb, cwb_ref, cbb_ref)
    ca = conv_silu(c_ref, extc, extpc, cwc_ref, cbc_ref)

    dtr = dt_ref[...] + dtb_ref[...]
    dt = jnp.maximum(dtr, 0.0) + jnp.log(1.0 + jnp.exp(-jnp.abs(dtr)))
    dt = jnp.where(valid, dt, 0.0)
    a = -jnp.exp(alog_ref[...])
    ri = lax.broadcasted_iota(jnp.int32, (q, q), 0)
    cj = lax.broadcasted_iota(jnp.int32, (q, q), 1)
    causal = ri >= cj
    tri = causal.astype(f32)
    acs = jnp.dot(tri, dt * a, preferred_element_type=f32, precision=lax.Precision.HIGHEST)
    acs_t = acs.T
    eacs = jnp.exp(acs)
    dte = jnp.exp(acs[q - 1:q, :] - acs)

    e = e_ref[...]
    dt_x = _expand_heads(dt, e)
    eacs_x = _expand_heads(eacs, e)
    dte_x = _expand_heads(dte, e)

    xdt = xa * dt_x
    xd_end = (xdt * dte_x).astype(MM)
    xdt_m = xdt.astype(MM)
    lane = lax.broadcasted_iota(jnp.int32, (q, LANES), 1)
    lo_half = lane < SSM_HEAD_DIM

    heads_per_group = SSM_HEADS // SSM_GROUPS
    for g in range(SSM_GROUPS):
        gs = slice(g * GROUP_WIDTH, (g + 1) * GROUP_WIDTH)
        bg = ba[:, g * SSM_STATE:(g + 1) * SSM_STATE]
        cg = ca[:, g * SSM_STATE:(g + 1) * SSM_STATE]
        cb = _dot_nt(cg, bg)
        h_in = state[g]
        y_off = _dot(cg, h_in) * eacs_x[:, gs]
        s_new = _dot_tn(bg, xd_end[:, gs])
        state[g] = h_in * eacs_x[q - 1:q, gs] + s_new
        for pr in range(heads_per_group // 2):
            h0 = g * heads_per_group + 2 * pr
            ms = []
            for hh in (h0, h0 + 1):
                diff = acs[:, hh:hh + 1] - acs_t[hh:hh + 1, :]
                lmat = jnp.exp(jnp.where(causal, diff, -jnp.inf))
                ms.append((cb * lmat).astype(MM))
            lhs = jnp.concatenate(ms, axis=1)
            xp = xdt_m[:, h0 * SSM_HEAD_DIM:(h0 + 2) * SSM_HEAD_DIM]
            zero = jnp.zeros_like(xp)
            rhs = jnp.concatenate([jnp.where(lo_half, xp, zero), jnp.where(lo_half, zero, xp)], axis=0)
            yd = jnp.dot(lhs, rhs, preferred_element_type=f32)
            cs = slice(h0 * SSM_HEAD_DIM, (h0 + 2) * SSM_HEAD_DIM)
            ybuf[:, cs] = yd + y_off[:, 2 * pr * SSM_HEAD_DIM:(2 * pr + 2) * SSM_HEAD_DIM]

    y = ybuf[...] + dsk_ref[...] * xa
    gz = y * _silu(z_ref[...])
    for g in range(SSM_GROUPS):
        gs = slice(g * GROUP_WIDTH, (g + 1) * GROUP_WIDTH)
        gg = gz[:, gs]
        ms = jnp.mean(gg * gg, axis=-1, keepdims=True)
        o_ref[:, gs] = (gg * lax.rsqrt(ms + EPS) * nw_ref[:, gs]).astype(o_ref.dtype)


def _ssd(proj, lead, bsz, tp, cwx, cwb, cwc, cbx, cbb, cbc, dtb, alog, dsk, nw):
    nc = tp // ROW_BLOCK
    q = ROW_BLOCK

    def seg(name):
        w = _DST[name][1]
        cbk = _col_block(name)
        return pl.BlockSpec((q, w), lambda b, c: (b * nc + c, cbk))

    def full(a):
        return pl.BlockSpec(a.shape, lambda b, c: (0, 0))

    small = (cwx, cwb, cwc, cbx, cbb, cbc, dtb, alog, dsk, nw)
    return pl.pallas_call(
        functools.partial(_ssd_kernel, lead),
        grid=(bsz, nc),
        in_specs=[seg("z"), seg("x"), seg("b"), seg("c"), seg("dt")] + [full(a) for a in small],
        out_specs=pl.BlockSpec((q, D_INNER), lambda b, c: (b * nc + c, 0)),
        out_shape=jax.ShapeDtypeStruct((bsz * tp, D_INNER), MM),
        scratch_shapes=[pltpu.VMEM((q + 8, D_INNER), jnp.float32),
                        pltpu.VMEM((q + 8, BC_WIDTH), jnp.float32),
                        pltpu.VMEM((q + 8, BC_WIDTH), jnp.float32),
                        pltpu.VMEM((q + 8, D_INNER), jnp.float32),
                        pltpu.VMEM((q + 8, BC_WIDTH), jnp.float32),
                        pltpu.VMEM((q + 8, BC_WIDTH), jnp.float32),
                        pltpu.VMEM((SSM_GROUPS, SSM_STATE, GROUP_WIDTH), jnp.float32),
                        pltpu.VMEM((q, D_INNER), jnp.float32),
                        pltpu.VMEM((LANES, D_INNER), jnp.bfloat16)],
        compiler_params=_params(("parallel", "arbitrary")),
        name="ssd",
    )(proj, proj, proj, proj, proj, *small)


def _split2(a):
    hi = a.astype(jnp.bfloat16)
    lo = (a - hi.astype(jnp.float32)).astype(jnp.bfloat16)
    return hi, lo


def _dot2(a, b):
    hi, lo = _split2(a)
    f = functools.partial(jnp.dot, preferred_element_type=jnp.float32)
    return f(hi, b) + f(lo, b)


def _rot_half_matrix(head_dim):
    half = head_dim // ROPE_FRACTION // 2
    rj = lax.broadcasted_iota(jnp.int32, (LANES, LANES), 0)
    ci = lax.broadcasted_iota(jnp.int32, (LANES, LANES), 1)
    cm = ci % head_dim
    neg = (cm < half) & (rj == ci + half)
    pos = (cm >= half) & (cm < 2 * half) & (rj == ci - half)
    return jnp.where(neg, -1.0, jnp.where(pos, 1.0, 0.0)).astype(jnp.bfloat16)


def _rope128(x, tab, half):
    cos = tab[:, 0:LANES]
    s_a = tab[:, LANES:2 * LANES]
    s_b = tab[:, 2 * LANES:3 * LANES]
    return x * cos + pltpu.roll(x, LANES - half, 1) * s_a + pltpu.roll(x, half, 1) * s_b


def _rms128(x, w, width):
    ms = jnp.sum(x * x, axis=-1, keepdims=True) * (1.0 / width)
    return x * lax.rsqrt(ms + EPS) * w


def _kprep_kernel(k_ref, v_ref, ik_ref, ta_ref, ti_ref, knw_ref, iknw_ref,
                  ko_ref, vo_ref, iklo_ref, ikhi_ref):
    ta = ta_ref[...]
    for h in range(ATTN_KV_HEADS):
        cs = slice(h * ATTN_HEAD_DIM, (h + 1) * ATTN_HEAD_DIM)
        kn = _rms128(k_ref[:, cs], knw_ref[...], ATTN_HEAD_DIM)
        ko_ref[:, cs] = _rope128(kn, ta, ATTN_HEAD_DIM // ROPE_FRACTION // 2).astype(ko_ref.dtype)
    vo_ref[...] = v_ref[...].astype(vo_ref.dtype)
    ikn = _rms128(ik_ref[...], iknw_ref[...], IDX_DIM)
    ikn = _rope128(ikn, ti_ref[...], IDX_DIM // ROPE_FRACTION // 2)
    iklo_ref[...] = ikn.astype(iklo_ref.dtype)
    ikhi_ref[...] = pltpu.roll(ikn, IDX_DIM, 1).astype(ikhi_ref.dtype)


def _kprep(proj, tab_a, tab_i, knw, iknw, tp):
    m = proj.shape[0]
    tr = _pick_tile(tp, 1408, 8)
    nt = tp // tr

    def seg(name):
        w = _DST[name][1]
        cbk = _col_block(name)
        return pl.BlockSpec((tr, w), lambda i: (i, cbk))

    tab = pl.BlockSpec((tr, 3 * LANES), lambda i: (i % nt, 0))
    vec = pl.BlockSpec((1, LANES), lambda i: (0, 0))
    row = lambda w: pl.BlockSpec((tr, w), lambda i: (i, 0))
    return pl.pallas_call(
        _kprep_kernel,
        grid=(m // tr,),
        in_specs=[seg("k"), seg("v"), seg("ik"), tab, tab, vec, vec],
        out_specs=[row(KV_WIDTH), row(KV_WIDTH), row(LANES), row(LANES)],
        out_shape=[jax.ShapeDtypeStruct((m, KV_WIDTH), MM), jax.ShapeDtypeStruct((m, KV_WIDTH), MM),
                   jax.ShapeDtypeStruct((m, LANES), MM), jax.ShapeDtypeStruct((m, LANES), MM)],
        compiler_params=_params(("parallel",)),
        name="kprep",
    )(proj, proj, proj, tab_a, tab_i, knw, iknw)


def _attn_kernel(lead, k_sel, tk, tkw, q_ref, iq_ref, iw_ref, az_ref, ta_ref, ti_ref, qnw_ref,
                 kn_ref, vb_ref, iklo_ref, ikhi_ref, o_ref, key_ref, keyt_ref, s_ref, qs_ref):
    f32 = jnp.float32
    tq = ROW_BLOCK
    sb = tk // tq
    ts = sb * tq
    nlb = tk // LANES
    qi = pl.program_id(1)
    si = qi // sb

    @pl.when(qi % sb == 0)
    def _():
        nch = si + 1
        ti = ti_ref[...]
        npair = IDX_HEADS // 2
        xi = jnp.concatenate([iq_ref[:, p * LANES:(p + 1) * LANES] for p in range(npair)], axis=0)
        roti = _dot2(xi, _rot_half_matrix(IDX_DIM))
        cos_i, sin_i = ti[:, 0:LANES], ti[:, 2 * LANES:3 * LANES] - ti[:, LANES:2 * LANES]
        iq_all = xi.reshape(npair, ts, LANES) * cos_i[None] + roti.reshape(npair, ts, LANES) * sin_i[None]
        iq_all = iq_all.astype(MM).reshape(npair * ts, LANES)
        iw_t = (iw_ref[...] * (IDX_HEADS ** -0.5 * IDX_DIM ** -0.5)).T
        ones_cnt = jnp.ones((LANES, LANES), jnp.bfloat16)
        qcol = si * ts + lax.broadcasted_iota(jnp.int32, (1, ts), 1)
        krow0 = lax.broadcasted_iota(jnp.int32, (tk, 1), 0)

        def score_chunk(j, carry):
            off = pl.multiple_of(j * tk, tk)
            ik2 = jnp.concatenate([iklo_ref[pl.ds(off, tk), :], ikhi_ref[pl.ds(off, tk), :]], axis=0)
            sc = None
            for half in range(2):
                p0 = half * (npair // 2)
                l_both = _dot_nt(ik2, iq_all[p0 * ts:(p0 + npair // 2) * ts])
                for pp in range(npair // 2):
                    cs = slice(pp * ts, (pp + 1) * ts)
                    p = p0 + pp
                    term = (iw_t[2 * p:2 * p + 1, :] * jnp.maximum(l_both[:tk, cs], 0.0)
                            + iw_t[2 * p + 1:2 * p + 2, :] * jnp.maximum(l_both[tk:, cs], 0.0))
                    sc = term if sc is None else sc + term
            bits = pltpu.bitcast(sc, jnp.int32)
            keys = bits ^ ((bits >> 31) & jnp.int32(0x7FFFFFFF))
            krow = off + krow0
            visible = (krow <= qcol) & (krow >= lead)
            keyt_ref[pl.ds(off, tk), :] = jnp.where(visible, keys, jnp.int32(INT_MIN))
            return carry

        lax.fori_loop(0, nch, score_chunk, 0)

        def count_rows(off, g0, g1, lane0, cand, acc):
            for g in range(g0, g1):
                kk = keyt_ref[pl.ds(pl.multiple_of(off + g * 8, 8), 8), lane0:]
                hit = jnp.where(kk >= cand[:, lane0:], 1.0, 0.0)
                if lane0:
                    hit = jnp.concatenate([jnp.zeros((8, lane0), f32), hit], axis=1)
                acc = acc + hit
            return acc

        def bit_step(bi, t):
            cand = t ^ lax.shift_left(jnp.int32(1), 31 - bi)
            acc = lax.fori_loop(
                0, nch - 1,
                lambda j, a: count_rows(pl.multiple_of(j * tk, tk), 0, tk // 8, 0, cand, a),
                jnp.zeros((8, ts), f32))
            off = pl.multiple_of((nch - 1) * tk, tk)
            for r in range(sb):
                acc = count_rows(off, r * (tq // 8), (r + 1) * (tq // 8), r * tq, cand, acc)
            cnt = jnp.sum(acc, axis=0, keepdims=True)
            return jnp.where(cnt >= float(k_sel), cand, t)

        t_s = lax.fori_loop(0, 32, bit_step, jnp.full((8, ts), INT_MIN, jnp.int32))
        thr = jnp.broadcast_to(jnp.maximum(t_s[0:1], jnp.int32(INT_MIN + 1)), (tk, ts))

        def bias_chunk(j, carry):
            off = pl.multiple_of(j * tk, tk)
            bias_t = jnp.where(keyt_ref[pl.ds(off, tk), :] >= thr, 0.0, NEG_BIG)
            key_ref[:, pl.ds(off, tk)] = pltpu.bitcast(bias_t.T, jnp.int32)
            return carry

        lax.fori_loop(0, nch, bias_chunk, 0)

        ta = ta_ref[...]
        cos_a, sin_a = ta[:, 0:LANES], ta[:, 2 * LANES:3 * LANES] - ta[:, LANES:2 * LANES]
        rot_m = _rot_half_matrix(ATTN_HEAD_DIM)
        scale = ATTN_HEAD_DIM ** -0.5 * LOG2E
        for g in range(ATTN_KV_HEADS):
            heads = range(g * KV_GROUP, (g + 1) * KV_GROUP)
            xs = jnp.concatenate([q_ref[:, h * ATTN_HEAD_DIM:(h + 1) * ATTN_HEAD_DIM] for h in heads], axis=0)
            ms = _dot2(xs * xs, ones_cnt) * (1.0 / ATTN_HEAD_DIM)
            xn = xs * lax.rsqrt(ms + EPS) * qnw_ref[...]
            rot = _dot2(xn, rot_m)
            qr = xn.reshape(KV_GROUP, ts, LANES) * cos_a[None] + rot.reshape(KV_GROUP, ts, LANES) * sin_a[None]
            qr = (qr * scale).astype(MM)
            for hh in range(KV_GROUP):
                for r in range(sb):
                    row0 = (g * KV_GROUP + hh) * tq
                    qs_ref[r, row0:row0 + tq, :] = qr[hh, r * tq:(r + 1) * tq, :]

    rows4 = KV_GROUP * tq
    brow = pl.multiple_of((qi % sb) * tq, tq)
    macc0 = jnp.full((rows4, LANES), NEG_BIG, f32)
    acc0 = jnp.zeros((rows4, 2 * ATTN_HEAD_DIM), f32)

    def logits_chunk(g, off, w, macc):
        cs = slice(g * ATTN_HEAD_DIM, (g + 1) * ATTN_HEAD_DIM)
        q4 = qs_ref[qi % sb, g * rows4:(g + 1) * rows4, :]
        s = _dot_nt(q4, kn_ref[pl.ds(off, w), cs])
        bias = pltpu.bitcast(key_ref[pl.ds(brow, tq), pl.ds(off, w)], f32)
        s = (s.reshape(KV_GROUP, tq, w) + bias[None]).reshape(rows4, w)
        s_ref[g % 2, :, pl.ds(off, w)] = s
        for c in range(w // LANES):
            macc = jnp.maximum(macc, s[:, c * LANES:(c + 1) * LANES])
        return macc

    def pv_chunk(g, m, off, w, acc):
        cs = slice(g * ATTN_HEAD_DIM, (g + 1) * ATTN_HEAD_DIM)
        p = jnp.exp2(s_ref[g % 2, :, pl.ds(off, w)] - m).astype(MM)
        v1 = jnp.concatenate([vb_ref[pl.ds(off, w), cs], jnp.ones((w, ATTN_HEAD_DIM), MM)], axis=1)
        return acc + jnp.dot(p, v1, preferred_element_type=f32)

    needed = qi * tq + tq
    tkm = tkw // 2
    nwide = needed // tkw
    nmid = (needed - nwide * tkw) // tkm
    base = nwide * tkw + nmid * tkm
    nnarrow = (needed - base + tk - 1) // tk

    def sweep(fn, carry):
        carry = lax.fori_loop(0, nwide, lambda j, c: fn(pl.multiple_of(j * tkw, tkw), tkw, c), carry)
        carry = lax.fori_loop(0, nmid, lambda j, c: fn(pl.multiple_of(nwide * tkw, tkm), tkm, c), carry)
        return lax.fori_loop(0, nnarrow, lambda j, c: fn(pl.multiple_of(base + j * tk, tk), tk, c), carry)

    macc = sweep(functools.partial(logits_chunk, 0), macc0)
    for g in range(ATTN_KV_HEADS):
        m = jnp.max(macc, axis=-1, keepdims=True)
        if g + 1 < ATTN_KV_HEADS:
            def both(off, w, carry, g=g, m=m):
                return pv_chunk(g, m, off, w, carry[0]), logits_chunk(g + 1, off, w, carry[1])
            acc, macc = sweep(both, (acc0, macc0))
        else:
            acc = sweep(functools.partial(pv_chunk, g, m), acc0)
        o = acc[:, :ATTN_HEAD_DIM] / acc[:, ATTN_HEAD_DIM:]
        for r in range(KV_GROUP):
            h = g * KV_GROUP + r
            hs = slice(h * ATTN_HEAD_DIM, (h + 1) * ATTN_HEAD_DIM)
            o_ref[:, hs] = (o[r * tq:(r + 1) * tq, :] * _silu(az_ref[:, hs])).astype(o_ref.dtype)


def _attn(proj, kn, vb, iklo, ikhi, tab_a, tab_i, qnw, lead, k_sel, bsz, tp):
    tq = ROW_BLOCK
    nq = tp // tq
    sb = _pick_tile(nq, 3, 1)
    tk = sb * tq
    tkw = ATTN_WIDE_CHUNKS * tk
    ns = nq // sb

    def seg(name):
        w = _DST[name][1]
        cbk = _col_block(name)
        return pl.BlockSpec((tq, w), lambda b, i: (b * nq + i, cbk))

    def seg_sb(name):
        w = _DST[name][1]
        cbk = _col_block(name)
        return pl.BlockSpec((tk, w), lambda b, i: (b * ns + i // sb, cbk), pipeline_mode=pl.Buffered(1))

    tab_sb = pl.BlockSpec((tk, 3 * LANES), lambda b, i: (i // sb, 0), pipeline_mode=pl.Buffered(1))
    vec = pl.BlockSpec((1, LANES), lambda b, i: (0, 0))
    keys = lambda w: pl.BlockSpec((tp, w), lambda b, i: (b, 0), pipeline_mode=pl.Buffered(1))
    return pl.pallas_call(
        functools.partial(_attn_kernel, lead, k_sel, tk, tkw),
        grid=(bsz, nq),
        in_specs=[seg_sb("q"), seg_sb("iq"), seg_sb("iw"), seg("az"), tab_sb, tab_sb, vec,
                  keys(KV_WIDTH), keys(KV_WIDTH), keys(LANES), keys(LANES)],
        out_specs=pl.BlockSpec((tq, ATTN_WIDTH), lambda b, i: (b * nq + i, 0)),
        out_shape=jax.ShapeDtypeStruct((bsz * tp, ATTN_WIDTH), MM),
        scratch_shapes=[pltpu.VMEM((tk, tp), jnp.int32),
                        pltpu.VMEM((tp, tk), jnp.int32),
                        pltpu.VMEM((2, KV_GROUP * tq, tp), jnp.float32),
                        pltpu.VMEM((sb, ATTN_HEADS * tq, ATTN_HEAD_DIM), MM)],
        compiler_params=_params(("parallel", "arbitrary")),
        name="attn",
    )(proj, proj, proj, proj, tab_a, tab_i, qnw, kn, vb, iklo, ikhi)


def _merge_kernel(l_ref, ys_ref, ya_ref, gs_ref, ga_ref, ws_ref, wa_ref, o_ref):
    del l_ref
    ys = jnp.dot(ys_ref[...], ws_ref[...], preferred_element_type=jnp.float32)
    ya = jnp.dot(ya_ref[...], wa_ref[...], preferred_element_type=jnp.float32)
    o_ref[...] = (_sigmoid(gs_ref[...]) * ys + _sigmoid(ga_ref[...]) * ya).astype(o_ref.dtype)


def _merge(l_arr, ys, ya, proj, ws, wa):
    m = ys.shape[0]
    tm = _pick_tile(m, 1056, 8)
    tn = 512
    gsb = _DST["gs"][0] // tn
    gab = _DST["ga"][0] // tn
    return pl.pallas_call(
        _merge_kernel,
        grid_spec=pltpu.PrefetchScalarGridSpec(
            num_scalar_prefetch=1, grid=(m // tm, D_MODEL // tn),
            in_specs=[pl.BlockSpec((tm, D_INNER), lambda i, j, l: (i, 0)),
                      pl.BlockSpec((tm, ATTN_WIDTH), lambda i, j, l: (i, 0)),
                      pl.BlockSpec((tm, tn), lambda i, j, l: (i, gsb + j)),
                      pl.BlockSpec((tm, tn), lambda i, j, l: (i, gab + j)),
                      pl.BlockSpec((None, D_INNER, tn), lambda i, j, l: (l[0], 0, j)),
                      pl.BlockSpec((None, ATTN_WIDTH, tn), lambda i, j, l: (l[0], 0, j))],
            out_specs=pl.BlockSpec((tm, tn), lambda i, j, l: (i, j))),
        out_shape=jax.ShapeDtypeStruct((m, D_MODEL), MM),
        compiler_params=_params(("parallel", "arbitrary")),
        name="merge",
    )(l_arr, ys, ya, proj, proj, ws, wa)


def _outproj_kernel(lead, tp, l_ref, mg_ref, h_ref, w_ref, o_ref):
    del l_ref
    tm = mg_ref.shape[0]
    upd = jnp.dot(mg_ref[...], w_ref[...], preferred_element_type=jnp.float32)
    rows = pl.program_id(0) * tm + lax.broadcasted_iota(jnp.int32, (tm, 1), 0)
    o_ref[...] = jnp.where(rows % tp >= lead, h_ref[...] + upd, 0.0)


def _outproj(l_arr, mg, h2, wo, lead, tp):
    m = mg.shape[0]
    tm = _pick_tile(m, 1056, 8)
    tn = 512
    return pl.pallas_call(
        functools.partial(_outproj_kernel, lead, tp),
        grid_spec=pltpu.PrefetchScalarGridSpec(
            num_scalar_prefetch=1, grid=(m // tm, D_MODEL // tn),
            in_specs=[pl.BlockSpec((tm, D_MODEL), lambda i, j, l: (i, 0)),
                      pl.BlockSpec((tm, tn), lambda i, j, l: (i, j)),
                      pl.BlockSpec((None, D_MODEL, tn), lambda i, j, l: (l[0], 0, j))],
            out_specs=pl.BlockSpec((tm, tn), lambda i, j, l: (i, j))),
        out_shape=jax.ShapeDtypeStruct((m, D_MODEL), jnp.float32),
        input_output_aliases={2: 0},
        compiler_params=_params(("parallel", "arbitrary")),
        name="outproj",
    )(l_arr, mg, h2, wo)


def _rope_table(pos, head_dim):
    rot = head_dim // ROPE_FRACTION
    half = rot // 2
    inv = ROPE_THETA ** (-jnp.arange(0, rot, 2, dtype=jnp.float32) / rot)
    ang = pos.astype(jnp.float32)[:, None] * inv[None, :]
    cos, sin = jnp.cos(ang), jnp.sin(ang)
    n = pos.shape[0]
    rest = head_dim - rot
    c = jnp.concatenate([cos, cos, jnp.ones((n, rest), jnp.float32)], axis=1)
    s_a = jnp.concatenate([-sin, jnp.zeros((n, head_dim - half), jnp.float32)], axis=1)
    s_b = jnp.concatenate([jnp.zeros((n, half), jnp.float32), sin, jnp.zeros((n, rest), jnp.float32)], axis=1)
    rep = LANES // head_dim
    return jnp.concatenate([jnp.tile(c, (1, rep)), jnp.tile(s_a, (1, rep)), jnp.tile(s_b, (1, rep))], axis=1)


def _pad_lanes(a, width=LANES):
    return jnp.pad(a, [(0, 0)] * (a.ndim - 1) + [(0, width - a.shape[-1])])


def _prep_w_in(w_in):
    depth, d, _ = w_in.shape
    parts = []
    run = None
    for name, (_, w) in _DST.items():
        s0, sw = _SRC[name]
        if sw == w and run is not None and run[1] == s0:
            run[1] = s0 + sw
            continue
        if run is not None:
            parts.append(w_in[:, :, run[0]:run[1]])
            run = None
        if sw == w:
            run = [s0, s0 + sw]
        else:
            parts.append(_pad_lanes(w_in[:, :, s0:s0 + sw], w))
    if run is not None:
        parts.append(w_in[:, :, run[0]:run[1]])
    used = sum(w for _, w in _DST.values())
    parts.append(jnp.zeros((depth, d, NP - used), w_in.dtype))
    return jnp.concatenate(parts, axis=-1).astype(MM)


def kernel(x, meta_tokens, norm_w, w_in, conv_w, conv_b, dt_bias, a_log, d_skip, ssm_norm_w,
           w_ssm_out, q_norm_w, k_norm_w, idx_k_norm_w, w_attn_out, w_out):
    bsz, seq, d = x.shape
    depth = norm_w.shape[0]
    assert d == D_MODEL and w_in.shape[-1] == N_IN
    t = seq + N_META
    lead = (-t) % ROW_BLOCK
    tp = t + lead
    k_sel = min(TOPK_MAX, seq // 4)

    meta = jnp.broadcast_to(meta_tokens[None].astype(x.dtype), (bsz, N_META, d))
    h = jnp.concatenate([jnp.zeros((bsz, lead, d), x.dtype), meta, x], axis=1).reshape(bsz * tp, d)

    pos = jnp.arange(tp) - lead
    tab_a = _rope_table(pos, ATTN_HEAD_DIM)
    tab_i = _rope_table(pos, IDX_DIM)

    wp = _prep_w_in(w_in)
    ws = w_ssm_out.astype(MM)
    wa = w_attn_out.astype(MM)
    wo = w_out.astype(MM)
    cw_x, cw_b, cw_c = (conv_w[:, :, :D_INNER], conv_w[:, :, D_INNER:D_INNER + BC_WIDTH],
                        conv_w[:, :, D_INNER + BC_WIDTH:])
    cb_x, cb_b, cb_c = (conv_b[:, None, :D_INNER], conv_b[:, None, D_INNER:D_INNER + BC_WIDTH],
                        conv_b[:, None, D_INNER + BC_WIDTH:])
    dtb = _pad_lanes(dt_bias)[:, None, :]
    alog = _pad_lanes(a_log)[:, None, :]
    dsk = jnp.repeat(d_skip, SSM_HEAD_DIM, axis=-1)[:, None, :]
    snw = ssm_norm_w[:, None, :]
    iknw = _pad_lanes(idx_k_norm_w)[:, None, :]
    qnw = q_norm_w[:, None, :]
    knw = k_norm_w[:, None, :]
    nw = norm_w[:, None, :]

    def layer(l, h):
        l_arr = jnp.reshape(l, (1,)).astype(jnp.int32)
        proj = _inproj(l_arr, h, nw[l], wp)
        ys = _ssd(proj, lead, bsz, tp, cw_x[l], cw_b[l], cw_c[l], cb_x[l], cb_b[l], cb_c[l],
                  dtb[l], alog[l], dsk[l], snw[l])
        kn, vb, iklo, ikhi = _kprep(proj, tab_a, tab_i, knw[l], iknw[l], tp)
        ya = _attn(proj, kn, vb, iklo, ikhi, tab_a, tab_i, qnw[l], lead, k_sel, bsz, tp)
        mg = _merge(l_arr, ys, ya, proj, ws, wa)
        return _outproj(l_arr, mg, h, wo, lead, tp)

    h = lax.fori_loop(0, depth, layer, h)
    return h.reshape(bsz, tp, d)[:, lead + N_META:]
```
